```python
import jax, jax.numpy as jnp
from jax import lax
import numpy as np

D_MODEL = 1024
BATCH = 8
SEQ = 2048
DEPTH = 4
DEC_BATCH = 128
DEC_SEQ = 4
PAST_LEN = 16384
PAGE_SIZE = 128

N_MIXERS = 4
N_POOL_L = (DEPTH + 3) // 4
N_CONV_L = (DEPTH + 2) // 4
N_GM_L = (DEPTH + 1) // 4
N_SC_L = DEPTH // 4
POOL_WINDOWS = (2, 4, 8, 16)
N_POOL_GROUPS = len(POOL_WINDOWS)
POOL_GROUP = D_MODEL // N_POOL_GROUPS
POOL_BUF = max(POOL_WINDOWS) - 1
CONV_WIDTH = 31
D_CONV = D_MODEL
CONV_BUF = CONV_WIDTH - 1
CHUNK = 128
D_GM = D_MODEL
N_GM_GROUPS = 4
GM_GROUP = D_GM // N_GM_GROUPS
SC_WIDTH = 3
SC_BUF = SC_WIDTH - 1
D_FF = -(-8 * D_MODEL // (3 * 256)) * 256
EPS = 1e-6

kernel_name = 'hybrid_pool_conv_gmlp_shortconv_decode_step'


def _rmsnorm(x, g):
    xf = x.astype(jnp.float32)
    y = xf * lax.rsqrt(jnp.mean(xf * xf, axis=-1, keepdims=True) + EPS)
    return (y * g.astype(jnp.float32)).astype(x.dtype)


def _layernorm(x, g, b):
    xf = x.astype(jnp.float32)
    mu = jnp.mean(xf, axis=-1, keepdims=True)
    xc = xf - mu
    y = xc * lax.rsqrt(jnp.mean(xc * xc, axis=-1, keepdims=True) + EPS)
    return (y * g.astype(jnp.float32) + b.astype(jnp.float32)).astype(x.dtype)


def _depthwise_causal(ext, w):
    return lax.conv_general_dilated(ext, w[:, None, :].astype(ext.dtype), window_strides=(1,), padding='VALID', dimension_numbers=('NWC', 'WIO', 'NWC'), feature_group_count=ext.shape[-1])


def _pool_mixer(h, buf, pos0, w, scale):
    B, L, D = h.shape
    ext = jnp.concatenate([buf.astype(h.dtype), h], axis=1)
    cs = jnp.cumsum(ext.astype(jnp.float32), axis=1)
    cs = jnp.concatenate([jnp.zeros_like(cs[:, :1]), cs], axis=1)
    end = cs[:, POOL_BUF + 1:]
    hf = h.astype(jnp.float32)
    pos = pos0 + jnp.arange(L)
    diffs = []
    for g, win in enumerate(POOL_WINDOWS):
        sl = slice(g * POOL_GROUP, (g + 1) * POOL_GROUP)
        start = cs[:, POOL_BUF + 1 - win: POOL_BUF + 1 - win + L, sl]
        cnt = jnp.minimum(win, pos + 1).astype(jnp.float32)[None, :, None]
        diffs.append((end[..., sl] - start) / cnt - hf[..., sl])
    d = jnp.stack(diffs, axis=2).astype(h.dtype)
    y = jnp.einsum('blgc,gce->blge', d, w).reshape(B, L, D)
    return y * scale, ext[:, L:]


def _conformer_conv(h, buf, w_in, b_in, dw, dw_b, ln_g, ln_b, w_out):
    L = h.shape[1]
    z = jnp.einsum('bld,de->ble', h, w_in) + b_in
    a, gate = jnp.split(z, 2, axis=-1)
    g = a * jax.nn.sigmoid(gate)
    ext = jnp.concatenate([buf.astype(g.dtype), g], axis=1)
    c = _depthwise_causal(ext, dw) + dw_b
    c = jax.nn.silu(_layernorm(c, ln_g, ln_b))
    return jnp.einsum('blc,cd->bld', c, w_out), ext[:, L:]


def _chunk_gmlp(h, w_in, ln_g, ln_b, w_s, b_s, w_out):
    B, L, _ = h.shape
    z = jax.nn.gelu(jnp.einsum('bld,de->ble', h, w_in))
    u, v = jnp.split(z, 2, axis=-1)
    v = _layernorm(v, ln_g, ln_b)
    c = min(L, CHUNK)
    n = L // c
    mask = jnp.tril(jnp.ones((c, c), dtype=bool))
    ws = jnp.where(mask[None], w_s[:, :c, :c], 0)
    vg = v.reshape(B, n, c, N_GM_GROUPS, GM_GROUP)
    mixed = jnp.einsum('gts,bnsgc->bntgc', ws, vg) + b_s[:, :c].T[None, None, :, :, None]
    y = u * mixed.reshape(B, L, D_GM)
    return jnp.einsum('ble,ed->bld', y, w_out), v[:, (n - 1) * c:]


def _short_conv(h, buf, w_in, conv_w, w_out):
    L = h.shape[1]
    z = jnp.einsum('bld,de->ble', h, w_in)
    bg, cg, xv = jnp.split(z, 3, axis=-1)
    cx = cg * xv
    ext = jnp.concatenate([buf.astype(cx.dtype), cx], axis=1)
    y = bg * _depthwise_causal(ext, conv_w)
    return jnp.einsum('ble,ed->bld', y, w_out), ext[:, L:]


def _swiglu(h, w_in, w_out):
    gate, up = jnp.split(jnp.einsum('bld,df->blf', h, w_in), 2, axis=-1)
    return jnp.einsum('blf,fd->bld', jax.nn.silu(gate) * up, w_out)


def _trunk(x, pool_buf, conv_buf, sc_buf, pos0, norm_mix, norm_ffn, norm_final, pool_w, pool_scale, conv_w_in, conv_b_in, conv_dw, conv_dw_b, conv_ln_g, conv_ln_b, conv_w_out, gm_w_in, gm_ln_g, gm_ln_b, gm_w_s, gm_b_s, gm_w_out, sc_w_in, sc_conv, sc_w_out, ffn_w_in, ffn_w_out):
    new_pool, new_conv, new_v, new_sc = [], [], [], []
    for i in range(DEPTH):
        m, j = i % N_MIXERS, i // N_MIXERS
        h = _rmsnorm(x, norm_mix[i])
        if m == 0:
            y, s = _pool_mixer(h, pool_buf[j], pos0, pool_w[j], pool_scale[j])
            new_pool.append(s)
        elif m == 1:
            y, s = _conformer_conv(h, conv_buf[j], conv_w_in[j], conv_b_in[j], conv_dw[j], conv_dw_b[j], conv_ln_g[j], conv_ln_b[j], conv_w_out[j])
            new_conv.append(s)
        elif m == 2:
            y, s = _chunk_gmlp(h, gm_w_in[j], gm_ln_g[j], gm_ln_b[j], gm_w_s[j], gm_b_s[j], gm_w_out[j])
            new_v.append(s)
        else:
            y, s = _short_conv(h, sc_buf[j], sc_w_in[j], sc_conv[j], sc_w_out[j])
            new_sc.append(s)
        x = x + y
        x = x + _swiglu(_rmsnorm(x, norm_ffn[i]), ffn_w_in[i], ffn_w_out[i])
    return _rmsnorm(x, norm_final), jnp.stack(new_pool), jnp.stack(new_conv), jnp.stack(new_v), jnp.stack(new_sc)


def setup_inputs(seed: int = 0) -> dict:
    key = jax.random.key(seed)
    ks = iter(jax.random.split(key, 40))
    f32 = jnp.float32

    def nrm(shape, scale):
        return jax.random.normal(next(ks), shape, f32) * scale

    def gain(shape):
        return 1.0 + nrm(shape, 0.1)

    D = D_MODEL
    return {
        'x_prompt': nrm((BATCH, SEQ, D), 1.0),
        'x_sample': nrm((DEC_BATCH, DEC_SEQ, D), 1.0),
        'state_pool': nrm((N_POOL_L, DEC_BATCH, POOL_BUF, D), 1.0),
        'state_conv': nrm((N_CONV_L, DEC_BATCH, CONV_BUF, D_CONV), 0.5),
        'state_shortconv': nrm((N_SC_L, DEC_BATCH, SC_BUF, D), 1.0),
        'norm_mix': gain((DEPTH, D)),
        'norm_ffn': gain((DEPTH, D)),
        'norm_final': gain((D,)),
        'pool_w': nrm((N_POOL_L, N_POOL_GROUPS, POOL_GROUP, POOL_GROUP), POOL_GROUP ** -0.5),
        'pool_scale': gain((N_POOL_L, D)),
        'conv_w_in': nrm((N_CONV_L, D, 2 * D_CONV), D ** -0.5),
        'conv_b_in': nrm((N_CONV_L, 2 * D_CONV), 0.02),
        'conv_dw': nrm((N_CONV_L, CONV_WIDTH, D_CONV), CONV_WIDTH ** -0.5),
        'conv_dw_b': nrm((N_CONV_L, D_CONV), 0.02),
        'conv_ln_g': gain((N_CONV_L, D_CONV)),
        'conv_ln_b': nrm((N_CONV_L, D_CONV), 0.02),
        'conv_w_out': nrm((N_CONV_L, D_CONV, D), D_CONV ** -0.5),
        'gm_w_in': nrm((N_GM_L, D, 2 * D_GM), D ** -0.5),
        'gm_ln_g': gain((N_GM_L, D_GM)),
        'gm_ln_b': nrm((N_GM_L, D_GM), 0.02),
        'gm_w_s': nrm((N_GM_L, N_GM_GROUPS, CHUNK, CHUNK), CHUNK ** -0.5),
        'gm_b_s': gain((N_GM_L, N_GM_GROUPS, CHUNK)),
        'gm_w_out': nrm((N_GM_L, D_GM, D), D_GM ** -0.5),
        'sc_w_in': nrm((N_SC_L, D, 3 * D), D ** -0.5),
        'sc_conv': nrm((N_SC_L, SC_WIDTH, D), SC_WIDTH ** -0.5),
        'sc_w_out': nrm((N_SC_L, D, D), D ** -0.5),
        'ffn_w_in': nrm((DEPTH, D, 2 * D_FF), D ** -0.5),
        'ffn_w_out': nrm((DEPTH, D_FF, D), D_FF ** -0.5),
    }


def reference(x_prompt, x_sample, state_pool, state_conv, state_shortconv, norm_mix, norm_ffn, norm_final, pool_w, pool_scale, conv_w_in, conv_b_in, conv_dw, conv_dw_b, conv_ln_g, conv_ln_b, conv_w_out, gm_w_in, gm_ln_g, gm_ln_b, gm_w_s, gm_b_s, gm_w_out, sc_w_in, sc_conv, sc_w_out, ffn_w_in, ffn_w_out):
    weights = (norm_mix, norm_ffn, norm_final, pool_w, pool_scale, conv_w_in, conv_b_in, conv_dw, conv_dw_b, conv_ln_g, conv_ln_b, conv_w_out, gm_w_in, gm_ln_g, gm_ln_b, gm_w_s, gm_b_s, gm_w_out, sc_w_in, sc_conv, sc_w_out, ffn_w_in, ffn_w_out)
    B = x_prompt.shape[0]
    dt = x_prompt.dtype
    pool0 = jnp.zeros((N_POOL_L, B, POOL_BUF, D_MODEL), dt)
    conv0 = jnp.zeros((N_CONV_L, B, CONV_BUF, D_CONV), dt)
    sc0 = jnp.zeros((N_SC_L, B, SC_BUF, D_MODEL), dt)
    y_prompt, pool_p, conv_p, v_p, sc_p = _trunk(x_prompt, pool0, conv0, sc0, 0, *weights)
    y_sample, pool_s, conv_s, v_s, sc_s = _trunk(x_sample, state_pool, state_conv, state_shortconv, PAST_LEN, *weights)
    return (y_prompt, y_sample, pool_p, pool_s, conv_p, conv_s, v_p, v_s, sc_p, sc_s)
```

```python
import functools

import jax
import jax.numpy as jnp
from jax import lax
from jax.experimental import pallas as pl
from jax.experimental.pallas import tpu as pltpu

F32 = jnp.float32
BF16 = jnp.bfloat16

EPS = 1e-6
POOL_WINDOWS = (2, 4, 8, 16)
POOL_BUF = max(POOL_WINDOWS) - 1
CONV_WIDTH = 31
CONV_BUF = CONV_WIDTH - 1
CHUNK = 128
N_GM_GROUPS = 4
SC_WIDTH = 3
SC_BUF = SC_WIDTH - 1
PAST_LEN = 16384

V7X_SUBLANES = 8
V7X_VMEM_BYTES = 64 * 1024 * 1024
VMEM_LIMIT = V7X_VMEM_BYTES - 8 * 1024 * 1024

TM = 512
POOL_HALO = 16
CONV_HALO = 32
SC_HALO = 8
CONV_ROWS = 32


def _params(n_grid):
    return pltpu.CompilerParams(
        dimension_semantics=("arbitrary",) * n_grid, vmem_limit_bytes=VMEM_LIMIT)


def _const_spec(shape):
    nd = len(shape)
    return pl.BlockSpec(shape, lambda *_: (0,) * nd, pipeline_mode=pl.Buffered(1))


def _rms(x, g):
    y = x * lax.rsqrt(jnp.mean(x * x, axis=-1, keepdims=True) + EPS)
    return y * g


def _ln(x, g, b):
    mu = jnp.mean(x, axis=-1, keepdims=True)
    xc = x - mu
    y = xc * lax.rsqrt(jnp.mean(xc * xc, axis=-1, keepdims=True) + EPS)
    return y * g + b


def _sigmoid(x):
    return 1.0 / (1.0 + jnp.exp(-x))


def _silu(x):
    return x * _sigmoid(x)


def _gelu_tanh(x):
    c = 0.7978845608028654
    return x * (0.5 * (1.0 + jnp.tanh(c * (x + 0.044715 * (x * x * x)))))


def _dot(a, b):
    return jnp.dot(a, b, preferred_element_type=F32)


def _ffn_kernel(x_ref, g_ref, win_ref, wout_ref, gf_ref, o_ref, *, d_ff, ff_chunk, final):
    x = x_ref[...]
    h = _rms(x, g_ref[...]).astype(BF16)
    acc = x
    for c0 in range(0, d_ff, ff_chunk):
        gate = _dot(h, win_ref[:, c0:c0 + ff_chunk])
        up = _dot(h, win_ref[:, d_ff + c0:d_ff + c0 + ff_chunk])
        act = (_silu(gate) * up).astype(BF16)
        acc = acc + _dot(act, wout_ref[c0:c0 + ff_chunk, :])
    if final:
        acc = _rms(acc, gf_ref[...])
    o_ref[...] = acc


def _ffn(x, g, w_in, w_out, g_final, *, final):
    m, d = x.shape
    d_ff = w_out.shape[0]
    tm = min(TM, m)
    kern = functools.partial(_ffn_kernel, d_ff=d_ff, ff_chunk=d_ff // 2, final=final)
    return pl.pallas_call(
        kern,
        grid=(m // tm,),
        in_specs=[
            pl.BlockSpec((tm, d), lambda i: (i, 0)),
            _const_spec((1, d)),
            _const_spec(w_in.shape),
            _const_spec(w_out.shape),
            _const_spec((1, d)),
        ],
        out_specs=pl.BlockSpec((tm, d), lambda i: (i, 0)),
        out_shape=jax.ShapeDtypeStruct((m, d), F32),
        compiler_params=_params(1),
        name="ffn_final" if final else "ffn",
    )(x, g, w_in, w_out, g_final)


def _carry_halo(ext_ref, halo, tm):
    ext_ref[0:halo, :] = ext_ref[tm:tm + halo, :]


def _pool_p_kernel(x_ref, g_ref, w_ref, sc_ref, o_ref, st_ref, ext_ref, *, tm, pos0):
    s = pl.program_id(1)
    grp = w_ref.shape[1]

    @pl.when(s == 0)
    def _():
        ext_ref[0:POOL_HALO, :] = jnp.zeros((POOL_HALO, ext_ref.shape[1]), F32)

    x = x_ref[...]
    h = _rms(x, g_ref[...])
    ext_ref[POOL_HALO:POOL_HALO + tm, :] = h
    pos = pos0 + s * tm + lax.broadcasted_iota(jnp.int32, (tm, 1), 0)
    ys = []
    for gi, win in enumerate(POOL_WINDOWS):
        c0 = gi * grp
        hg = h[:, c0:c0 + grp]
        acc = hg
        for k in range(1, win):
            acc = acc + ext_ref[POOL_HALO - k:POOL_HALO - k + tm, c0:c0 + grp]
        cnt = jnp.minimum(win, pos + 1).astype(F32)
        d = acc / cnt - hg
        ys.append(_dot(d.astype(BF16), w_ref[gi]))
    y = jnp.concatenate(ys, axis=1)
    o_ref[...] = x + y * sc_ref[...]

    @pl.when(s == pl.num_programs(1) - 1)
    def _():
        st_ref[...] = ext_ref[POOL_HALO + tm - POOL_BUF:POOL_HALO + tm, :]

    _carry_halo(ext_ref, POOL_HALO, tm)


def _conv_p_kernel(x_ref, g_ref, win_ref, bin_ref, dw_ref, dwb_ref, lng_ref, lnb_ref, wout_ref,
                   o_ref, st_ref, win3_ref, c_ref, *, tm):
    s = pl.program_id(1)
    dc = wout_ref.shape[0]
    n_chunks = tm // CONV_ROWS
    wrows = CONV_HALO + CONV_ROWS
    assert CONV_ROWS >= CONV_HALO

    @pl.when(s == 0)
    def _():
        win3_ref[0, 0:CONV_HALO, :] = jnp.zeros((CONV_HALO, dc), F32)

    x = x_ref[...]
    h = _rms(x, g_ref[...]).astype(BF16)
    z = _dot(h, win_ref[...]) + bin_ref[...]
    glu = z[:, :dc] * _sigmoid(z[:, dc:])
    for i in range(n_chunks):
        r1 = (i + 1) * CONV_ROWS
        win3_ref[i, CONV_HALO:wrows, :] = glu[r1 - CONV_ROWS:r1, :]
        if i + 1 < n_chunks:
            win3_ref[i + 1, 0:CONV_HALO, :] = glu[r1 - CONV_HALO:r1, :]

    base = CONV_HALO - CONV_BUF

    def body(i, carry):
        acc = jnp.zeros((CONV_ROWS, dc), F32) + dwb_ref[...]
        for k in range(CONV_WIDTH):
            acc = acc + win3_ref[i, base + k:base + k + CONV_ROWS, :] * dw_ref[k:k + 1, :]
        c_ref[pl.ds(pl.multiple_of(i * CONV_ROWS, CONV_ROWS), CONV_ROWS), :] = acc
        return carry

    lax.fori_loop(0, n_chunks, body, 0)
    c = _silu(_ln(c_ref[...], lng_ref[...], lnb_ref[...]))
    o_ref[...] = x + _dot(c.astype(BF16), wout_ref[...])

    @pl.when(s == pl.num_programs(1) - 1)
    def _():
        st_ref[...] = win3_ref[n_chunks - 1, wrows - CONV_BUF:wrows, :]

    win3_ref[0, 0:CONV_HALO, :] = win3_ref[n_chunks - 1, wrows - CONV_HALO:wrows, :]


def _gm_p_kernel(x_ref, g_ref, win_ref, lng_ref, lnb_ref, ws_ref, bs_ref, wout_ref,
                 o_ref, v_ref, *, tm):
    s = pl.program_id(1)
    dg = wout_ref.shape[0]
    grp = dg // N_GM_GROUPS
    x = x_ref[...]
    h = _rms(x, g_ref[...]).astype(BF16)
    z = _gelu_tanh(_dot(h, win_ref[...]))
    u = z[:, :dg]
    v = _ln(z[:, dg:], lng_ref[...], lnb_ref[...])
    vb = v.astype(BF16)
    row = lax.broadcasted_iota(jnp.int32, (CHUNK, CHUNK), 0)
    col = lax.broadcasted_iota(jnp.int32, (CHUNK, CHUNK), 1)
    causal = col <= row
    mixed_cols = []
    for gi in range(N_GM_GROUPS):
        wsg = jnp.where(causal, ws_ref[gi], 0.0).astype(BF16)
        bias = bs_ref[:, gi:gi + 1]
        parts = []
        for r0 in range(0, tm, CHUNK):
            parts.append(_dot(wsg, vb[r0:r0 + CHUNK, gi * grp:(gi + 1) * grp]) + bias)
        mixed_cols.append(jnp.concatenate(parts, axis=0))
    mixed = jnp.concatenate(mixed_cols, axis=1)
    y = (u * mixed).astype(BF16)
    o_ref[...] = x + _dot(y, wout_ref[...])

    @pl.when(s == pl.num_programs(1) - 1)
    def _():
        v_ref[...] = v[tm - CHUNK:, :]


def _sc_p_kernel(x_ref, g_ref, win_ref, cw_ref, wout_ref, o_ref, st_ref, ext_ref, *, tm):
    s = pl.program_id(1)
    d = wout_ref.shape[0]

    @pl.when(s == 0)
    def _():
        ext_ref[0:SC_HALO, :] = jnp.zeros((SC_HALO, d), F32)

    x = x_ref[...]
    h = _rms(x, g_ref[...]).astype(BF16)
    z = _dot(h, win_ref[...])
    cx = z[:, d:2 * d] * z[:, 2 * d:]
    ext_ref[SC_HALO:SC_HALO + tm, :] = cx
    conv = cx * cw_ref[SC_WIDTH - 1:SC_WIDTH, :]
    for k in range(SC_WIDTH - 1):
        off = SC_HALO - (SC_WIDTH - 1) + k
        conv = conv + ext_ref[off:off + tm, :] * cw_ref[k:k + 1, :]
    y = (z[:, :d] * conv).astype(BF16)
    o_ref[...] = x + _dot(y, wout_ref[...])

    @pl.when(s == pl.num_programs(1) - 1)
    def _():
        st_ref[...] = ext_ref[SC_HALO + tm - SC_BUF:SC_HALO + tm, :]

    _carry_halo(ext_ref, SC_HALO, tm)


def _prompt_mixer(kern, x, consts, *, state_rows, scratch, name):
    b, s, d = x.shape
    tm = min(TM, s)
    return pl.pallas_call(
        functools.partial(kern, tm=tm),
        grid=(b, s // tm),
        in_specs=[pl.BlockSpec((None, tm, d), lambda i, j: (i, j, 0))]
        + [_const_spec(c.shape) for c in consts],
        out_specs=[
            pl.BlockSpec((None, tm, d), lambda i, j: (i, j, 0)),
            pl.BlockSpec((None, state_rows, d), lambda i, j: (i, 0, 0)),
        ],
        out_shape=[
            jax.ShapeDtypeStruct((b, s, d), F32),
            jax.ShapeDtypeStruct((b, state_rows, d), F32),
        ],
        scratch_shapes=[pltpu.VMEM(shape, F32) for shape in scratch(tm, d)],
        compiler_params=_params(2),
        name=name,
    )(x, *consts)


def _slab(ref, j, d):
    return ref[:, j * d:(j + 1) * d]


def _pool_s_kernel(x_ref, st_ref, g_ref, w_ref, sc_ref, o_ref, nst_ref, d_ref, *, n_new, pos0):
    nb = x_ref.shape[0]
    d = g_ref.shape[1]
    grp = w_ref.shape[1]
    xs = [_slab(x_ref, l, d) for l in range(n_new)]
    hs = [_rms(x, g_ref[...]) for x in xs]

    def ext(j, c0):
        if j < POOL_BUF:
            return st_ref[:, j * d + c0:j * d + c0 + grp]
        return hs[j - POOL_BUF][:, c0:c0 + grp]

    for l in range(n_new):
        for gi, win in enumerate(POOL_WINDOWS):
            c0 = gi * grp
            acc = ext(POOL_BUF + l, c0)
            for k in range(1, win):
                acc = acc + ext(POOL_BUF + l - k, c0)
            cnt = float(min(win, pos0 + l + 1))
            d_ref[l * nb:(l + 1) * nb, c0:c0 + grp] = acc / cnt - hs[l][:, c0:c0 + grp]
    db = d_ref[...].astype(BF16)
    y = jnp.concatenate(
        [_dot(db[:, gi * grp:(gi + 1) * grp], w_ref[gi]) for gi in range(len(POOL_WINDOWS))], axis=1)
    y = y * sc_ref[...]
    for l in range(n_new):
        o_ref[l * nb:(l + 1) * nb, :] = xs[l] + y[l * nb:(l + 1) * nb, :]
    for j in range(POOL_BUF):
        src = j + n_new
        nst_ref[:, j * d:(j + 1) * d] = _slab(st_ref, src, d) if src < POOL_BUF else hs[src - POOL_BUF]


def _conv_s_kernel(x_ref, st_ref, g_ref, win_ref, bin_ref, dw_ref, dwb_ref, lng_ref, lnb_ref,
                   wout_ref, o_ref, nst_ref, g_scr, c_scr, *, n_new):
    nb = st_ref.shape[0]
    dc = wout_ref.shape[0]
    x = x_ref[...]
    h = _rms(x, g_ref[...]).astype(BF16)
    z = _dot(h, win_ref[...]) + bin_ref[...]
    g_scr[...] = z[:, :dc] * _sigmoid(z[:, dc:])

    def ext(j):
        if j < CONV_BUF:
            return _slab(st_ref, j, dc)
        return g_scr[(j - CONV_BUF) * nb:(j - CONV_BUF + 1) * nb, :]

    for l in range(n_new):
        acc = jnp.zeros((nb, dc), F32) + dwb_ref[...]
        for k in range(CONV_WIDTH):
            acc = acc + ext(l + k) * dw_ref[k:k + 1, :]
        c_scr[l * nb:(l + 1) * nb, :] = acc
    c = _silu(_ln(c_scr[...], lng_ref[...], lnb_ref[...]))
    o_ref[...] = x + _dot(c.astype(BF16), wout_ref[...])
    for j in range(CONV_BUF):
        nst_ref[:, j * dc:(j + 1) * dc] = ext(j + n_new)


def _gm_s_kernel(x_ref, g_ref, win_ref, lng_ref, lnb_ref, wrow_ref, brow_ref, wout_ref,
                 o_ref, v_ref, *, n_new, nb):
    dg = wout_ref.shape[0]
    x = x_ref[...]
    h = _rms(x, g_ref[...]).astype(BF16)
    z = _gelu_tanh(_dot(h, win_ref[...]))
    u = z[:, :dg]
    v = _ln(z[:, dg:], lng_ref[...], lnb_ref[...])
    mixed = []
    for t in range(n_new):
        acc = jnp.zeros((nb, dg), F32) + brow_ref[t:t + 1, :]
        for s in range(t + 1):
            acc = acc + wrow_ref[t * n_new + s:t * n_new + s + 1, :] * v[s * nb:(s + 1) * nb, :]
        mixed.append(acc)
    y = (u * jnp.concatenate(mixed, axis=0)).astype(BF16)
    o_ref[...] = x + _dot(y, wout_ref[...])
    for l in range(n_new):
        v_ref[:, l * dg:(l + 1) * dg] = v[l * nb:(l + 1) * nb, :]


def _sc_s_kernel(x_ref, st_ref, g_ref, win_ref, cw_ref, wout_ref, o_ref, nst_ref, *, n_new):
    nb = st_ref.shape[0]
    d = wout_ref.shape[0]
    x = x_ref[...]
    h = _rms(x, g_ref[...]).astype(BF16)
    z = _dot(h, win_ref[...])
    cx = z[:, d:2 * d] * z[:, 2 * d:]

    def ext(j):
        if j < SC_BUF:
            return _slab(st_ref, j, d)
        return cx[(j - SC_BUF) * nb:(j - SC_BUF + 1) * nb, :]

    convs = []
    for l in range(n_new):
        acc = ext(l) * cw_ref[0:1, :]
        for k in range(1, SC_WIDTH):
            acc = acc + ext(l + k) * cw_ref[k:k + 1, :]
        convs.append(acc)
    y = (z[:, :d] * jnp.concatenate(convs, axis=0)).astype(BF16)
    o_ref[...] = x + _dot(y, wout_ref[...])
    for j in range(SC_BUF):
        nst_ref[:, j * d:(j + 1) * d] = ext(j + n_new)


def _single_step(kern, operands, out_shape, scratch, name):
    return pl.pallas_call(
        kern,
        out_shape=out_shape,
        scratch_shapes=scratch,
        compiler_params=pltpu.CompilerParams(vmem_limit_bytes=VMEM_LIMIT),
        name=name,
    )(*operands)


def kernel(x_prompt, x_sample, state_pool, state_conv, state_shortconv, norm_mix, norm_ffn, norm_final, pool_w, pool_scale, conv_w_in, conv_b_in, conv_dw, conv_dw_b, conv_ln_g, conv_ln_b, conv_w_out, gm_w_in, gm_ln_g, gm_ln_b, gm_w_s, gm_b_s, gm_w_out, sc_w_in, sc_conv, sc_w_out, ffn_w_in, ffn_w_out):
    bp, sp, d = x_prompt.shape
    nb, n_new, _ = x_sample.shape
    depth = norm_mix.shape[0]
    assert depth == 4 and pool_w.shape[0] == 1, "one layer of each mixer type"
    assert sp % CHUNK == 0 and n_new < CHUNK and PAST_LEN % CHUNK == 0

    row = lambda a: a.reshape(1, -1)
    bf = lambda a: a.astype(BF16)
    gfin = row(norm_final)
    ffn_in, ffn_out = bf(ffn_w_in), bf(ffn_w_out)

    pool_consts = (row(norm_mix[0]), bf(pool_w[0]), row(pool_scale[0]))
    conv_consts = (row(norm_mix[1]), bf(conv_w_in[0]), row(conv_b_in[0]), conv_dw[0], row(conv_dw_b[0]),
                   row(conv_ln_g[0]), row(conv_ln_b[0]), bf(conv_w_out[0]))
    gm_head = (row(norm_mix[2]), bf(gm_w_in[0]), row(gm_ln_g[0]), row(gm_ln_b[0]))
    gm_out = bf(gm_w_out[0])
    sc_consts = (row(norm_mix[3]), bf(sc_w_in[0]), sc_conv[0], bf(sc_w_out[0]))

    def ffn(x2d, i):
        return _ffn(x2d, row(norm_ffn[i]), ffn_in[i], ffn_out[i], gfin, final=(i == depth - 1))

    def ffn_p(x, i):
        return ffn(x.reshape(bp * sp, d), i).reshape(bp, sp, d)

    x, pool_p = _prompt_mixer(
        functools.partial(_pool_p_kernel, pos0=0), x_prompt, pool_consts, state_rows=POOL_BUF,
        scratch=lambda tm, dd: [(POOL_HALO + tm, dd)], name="pool_p")
    x = ffn_p(x, 0)
    x, conv_p = _prompt_mixer(
        _conv_p_kernel, x, conv_consts, state_rows=CONV_BUF,
        scratch=lambda tm, dd: [(tm // CONV_ROWS, CONV_HALO + CONV_ROWS, dd), (tm, dd)], name="conv_p")
    x = ffn_p(x, 1)
    x, v_p = _prompt_mixer(
        _gm_p_kernel, x, gm_head + (gm_w_s[0], gm_b_s[0].T, gm_out), state_rows=CHUNK,
        scratch=lambda tm, dd: [], name="gm_p")
    x = ffn_p(x, 2)
    x, sc_p = _prompt_mixer(
        _sc_p_kernel, x, sc_consts, state_rows=SC_BUF,
        scratch=lambda tm, dd: [(SC_HALO + tm, dd)], name="sc_p")
    y_prompt = ffn_p(x, 3)

    m = nb * n_new
    xs2 = x_sample.reshape(nb, n_new * d)
    tm_shape = jax.ShapeDtypeStruct((m, d), F32)
    xs, pool_s = _single_step(
        functools.partial(_pool_s_kernel, n_new=n_new, pos0=PAST_LEN),
        (xs2, state_pool[0].reshape(nb, POOL_BUF * d)) + pool_consts,
        [tm_shape, jax.ShapeDtypeStruct((nb, POOL_BUF * d), F32)],
        [pltpu.VMEM((m, d), F32)], "pool_s")
    xs = ffn(xs, 0)
    xs, conv_s = _single_step(
        functools.partial(_conv_s_kernel, n_new=n_new),
        (xs, state_conv[0].reshape(nb, CONV_BUF * d)) + conv_consts,
        [tm_shape, jax.ShapeDtypeStruct((nb, CONV_BUF * d), F32)],
        [pltpu.VMEM((m, d), F32), pltpu.VMEM((m, d), F32)], "conv_s")
    xs = ffn(xs, 1)
    grp = d // N_GM_GROUPS
    wrow = jnp.repeat(jnp.transpose(gm_w_s[0][:, :n_new, :n_new], (1, 2, 0)).reshape(n_new * n_new, N_GM_GROUPS), grp, axis=1)
    brow = jnp.repeat(gm_b_s[0][:, :n_new].T, grp, axis=1)
    xs, v_s = _single_step(
        functools.partial(_gm_s_kernel, n_new=n_new, nb=nb),
        (xs,) + gm_head + (wrow, brow, gm_out),
        [tm_shape, jax.ShapeDtypeStruct((nb, n_new * d), F32)],
        [], "gm_s")
    xs = ffn(xs, 2)
    xs, sc_s = _single_step(
        functools.partial(_sc_s_kernel, n_new=n_new),
        (xs, state_shortconv[0].reshape(nb, SC_BUF * d)) + sc_consts,
        [tm_shape, jax.ShapeDtypeStruct((nb, SC_BUF * d), F32)],
        [], "sc_s")
    ys = ffn(xs, 3)
    y_sample = jnp.transpose(ys.reshape(n_new, nb, d), (1, 0, 2))

    return (y_prompt, y_sample,
            pool_p[None], pool_s.reshape(1, nb, POOL_BUF, d),
            conv_p[None], conv_s.reshape(1, nb, CONV_BUF, d),
            v_p[None], v_s.reshape(1, nb, n_new, d),
            sc_p[None], sc_s.reshape(1, nb, SC_BUF, d))
```

```python
import functools

import jax
import jax.numpy as jnp
from jax import lax
from jax.experimental import pallas as pl
from jax.experimental.pallas import tpu as pltpu

F32 = jnp.float32
BF16 = jnp.bfloat16

EPS = 1e-6
POOL_WINDOWS = (2, 4, 8, 16)
POOL_BUF = max(POOL_WINDOWS) - 1
CONV_WIDTH = 31
CONV_BUF = CONV_WIDTH - 1
CHUNK = 128
N_GM_GROUPS = 4
SC_WIDTH = 3
SC_BUF = SC_WIDTH - 1
PAST_LEN = 16384

V7X_SUBLANES = 8
V7X_LANES = 128
V7X_VMEM_BYTES = 64 * 1024 * 1024
VMEM_LIMIT = V7X_VMEM_BYTES - 8 * 1024 * 1024

TM = 512
POOL_HALO = 16
CONV_HALO = 32
SC_HALO = 8
CONV_UNROLL = 4


def _params(n_grid):
    return pltpu.CompilerParams(
        dimension_semantics=("arbitrary",) * n_grid, vmem_limit_bytes=VMEM_LIMIT)


def _const_spec(shape):
    nd = len(shape)
    return pl.BlockSpec(shape, lambda *_: (0,) * nd, pipeline_mode=pl.Buffered(1))


def _rms(x, g):
    y = x * lax.rsqrt(jnp.mean(x * x, axis=-1, keepdims=True) + EPS)
    return y * g


def _ln(x, g, b):
    mu = jnp.mean(x, axis=-1, keepdims=True)
    xc = x - mu
    y = xc * lax.rsqrt(jnp.mean(xc * xc, axis=-1, keepdims=True) + EPS)
    return y * g + b


def _sigmoid(x):
    return 0.5 * (1.0 + jnp.tanh(0.5 * x))


def _silu(x):
    return x * _sigmoid(x)


def _gelu_tanh(x):
    c = 0.7978845608028654
    return x * (0.5 * (1.0 + jnp.tanh(c * (x + 0.044715 * (x * x * x)))))


def _dot(a, b):
    return jnp.dot(a, b, preferred_element_type=F32)


def _ffn_kernel(x_ref, g_ref, win_ref, wout_ref, gf_ref, o_ref, *, d_ff, ff_chunk, final):
    x = x_ref[...]
    h = _rms(x, g_ref[...]).astype(BF16)
    acc = x
    for c0 in range(0, d_ff, ff_chunk):
        gate = _dot(h, win_ref[:, c0:c0 + ff_chunk])
        up = _dot(h, win_ref[:, d_ff + c0:d_ff + c0 + ff_chunk])
        act = (_silu(gate) * up).astype(BF16)
        acc = acc + _dot(act, wout_ref[c0:c0 + ff_chunk, :])
    if final:
        acc = _rms(acc, gf_ref[...])
    o_ref[...] = acc


def _layer_spec(shape, layer):
    nd = len(shape)
    return pl.BlockSpec((None,) + tuple(shape[1:]), lambda *_: (layer,) + (0,) * (nd - 1),
                        pipeline_mode=pl.Buffered(1))


def _ffn(x, g, w_in, w_out, g_final, *, layer, final):
    m, d = x.shape
    d_ff = w_out.shape[1]
    tm = min(TM, m)
    kern = functools.partial(_ffn_kernel, d_ff=d_ff, ff_chunk=d_ff // 2, final=final)
    return pl.pallas_call(
        kern,
        grid=(m // tm,),
        in_specs=[
            pl.BlockSpec((tm, d), lambda i: (i, 0)),
            _layer_spec(g.shape, layer),
            _layer_spec(w_in.shape, layer),
            _layer_spec(w_out.shape, layer),
            _const_spec((1, d)),
        ],
        out_specs=pl.BlockSpec((tm, d), lambda i: (i, 0)),
        out_shape=jax.ShapeDtypeStruct((m, d), F32),
        compiler_params=_params(1),
        name="ffn_final" if final else "ffn",
    )(x, g, w_in, w_out, g_final)


def _carry_halo(ext_ref, halo, tm):
    ext_ref[0:halo, :] = ext_ref[tm:tm + halo, :]


def _pool_p_kernel(x_ref, g_ref, w_ref, sc_ref, o_ref, st_ref, ext_ref, *, tm, pos0):
    s = pl.program_id(1)
    grp = w_ref.shape[1]

    @pl.when(s == 0)
    def _():
        ext_ref[0:POOL_HALO, :] = jnp.zeros((POOL_HALO, ext_ref.shape[1]), F32)

    x = x_ref[...]
    h = _rms(x, g_ref[...])
    ext_ref[POOL_HALO:POOL_HALO + tm, :] = h
    pos = pos0 + s * tm + lax.broadcasted_iota(jnp.int32, (tm, 1), 0)
    ys = []
    for gi, win in enumerate(POOL_WINDOWS):
        c0 = gi * grp
        hg = h[:, c0:c0 + grp]
        acc = hg
        for k in range(1, win):
            acc = acc + ext_ref[POOL_HALO - k:POOL_HALO - k + tm, c0:c0 + grp]
        cnt = jnp.minimum(win, pos + 1).astype(F32)
        d = acc / cnt - hg
        ys.append(_dot(d.astype(BF16), w_ref[gi]))
    y = jnp.concatenate(ys, axis=1)
    o_ref[...] = x + y * sc_ref[...]

    @pl.when(s == pl.num_programs(1) - 1)
    def _():
        st_ref[...] = ext_ref[POOL_HALO + tm - POOL_BUF:POOL_HALO + tm, :]

    _carry_halo(ext_ref, POOL_HALO, tm)


def _conv_p_kernel(x_ref, g_ref, win_ref, bin_ref, dw_ref, dwb_ref, lng_ref, lnb_ref, wout_ref,
                   o_ref, st_ref, glu_ref, ext_ref, c_ref, *, nb, ts):
    s = pl.program_id(0)
    d = x_ref.shape[2]
    n_lt, _, lane = glu_ref.shape
    pitch = glu_ref.shape[1] // nb

    @pl.when(s == 0)
    def _():
        ext_ref[:, 0:CONV_HALO * nb, :] = jnp.zeros((n_lt, CONV_HALO * nb, lane), F32)

    x = x_ref[...].reshape(nb * ts, d)
    h = _rms(x, g_ref[...]).astype(BF16)
    z = _dot(h, win_ref[...]) + bin_ref[...]
    dc = n_lt * lane
    glu = z[:, :dc] * _sigmoid(z[:, dc:])
    base = CONV_HALO - CONV_BUF
    for j in range(n_lt):
        lanes = slice(j * lane, (j + 1) * lane)
        for b in range(nb):
            glu_ref[j, b * pitch:b * pitch + ts, :] = glu[b * ts:(b + 1) * ts, lanes]
        for t in range(ts):
            ext_ref[j, (CONV_HALO + t) * nb:(CONV_HALO + t + 1) * nb, :] = (
                glu_ref[j, pl.ds(t, nb, stride=pitch), :])

        w = [jnp.broadcast_to(dw_ref[k:k + 1, lanes], (nb, lane)) for k in range(CONV_WIDTH)]
        bias = jnp.broadcast_to(dwb_ref[:, lanes], (nb, lane))

        def body(i, carry, j=j, w=w, bias=bias):
            for u in range(CONV_UNROLL):
                t = i * CONV_UNROLL + u
                acc = [bias, None]
                for k in range(CONV_WIDTH):
                    r = pl.multiple_of((t + base + k) * nb, nb)
                    term = ext_ref[j, pl.ds(r, nb), :] * w[k]
                    acc[k % 2] = term if acc[k % 2] is None else acc[k % 2] + term
                c_ref[j, pl.ds(pl.multiple_of(t * nb, nb), nb), :] = acc[0] + acc[1]
            return carry

        lax.fori_loop(0, ts // CONV_UNROLL, body, 0)

    c = jnp.concatenate(
        [jnp.concatenate([c_ref[j, pl.ds(b, ts, stride=nb), :] for b in range(nb)], axis=0)
         for j in range(n_lt)], axis=1)
    c = _silu(_ln(c, lng_ref[...], lnb_ref[...]))
    o_ref[...] = (x + _dot(c.astype(BF16), wout_ref[...])).reshape(nb, ts, d)

    @pl.when(s == pl.num_programs(0) - 1)
    def _():
        for b in range(nb):
            for j in range(n_lt):
                st_ref[b, :, j * lane:(j + 1) * lane] = (
                    glu_ref[j, b * pitch + ts - CONV_BUF:b * pitch + ts, :])

    ext_ref[:, 0:CONV_HALO * nb, :] = ext_ref[:, ts * nb:(ts + CONV_HALO) * nb, :]


def _conv_p(x, consts):
    nb, s, d = x.shape
    dc = consts[-1].shape[0]
    assert nb == V7X_SUBLANES, "the time-major copy puts one sequence on each sublane"
    ts = TM // nb
    assert s % ts == 0 and ts >= CONV_HALO
    return pl.pallas_call(
        functools.partial(_conv_p_kernel, nb=nb, ts=ts),
        grid=(s // ts,),
        in_specs=[pl.BlockSpec((nb, ts, d), lambda j: (0, j, 0))] + [_const_spec(c.shape) for c in consts],
        out_specs=[
            pl.BlockSpec((nb, ts, d), lambda j: (0, j, 0)),
            pl.BlockSpec((nb, CONV_BUF, dc), lambda j: (0, 0, 0)),
        ],
        out_shape=[
            jax.ShapeDtypeStruct((nb, s, d), F32),
            jax.ShapeDtypeStruct((nb, CONV_BUF, dc), F32),
        ],
        scratch_shapes=[
            pltpu.VMEM((dc // V7X_LANES, nb * (ts + V7X_SUBLANES), V7X_LANES), F32),
            pltpu.VMEM((dc // V7X_LANES, (CONV_HALO + ts) * nb, V7X_LANES), F32),
            pltpu.VMEM((dc // V7X_LANES, ts * nb, V7X_LANES), F32),
        ],
        compiler_params=_params(1),
        name="conv_p",
    )(x, *consts)


def _gm_p_kernel(x_ref, g_ref, win_ref, lng_ref, lnb_ref, ws_ref, bs_ref, wout_ref,
                 o_ref, v_ref, *, tm):
    s = pl.program_id(1)
    dg = wout_ref.shape[0]
    grp = dg // N_GM_GROUPS
    x = x_ref[...]
    h = _rms(x, g_ref[...]).astype(BF16)
    z = _gelu_tanh(_dot(h, win_ref[...]))
    u = z[:, :dg]
    v = _ln(z[:, dg:], lng_ref[...], lnb_ref[...])
    vb = v.astype(BF16)
    row = lax.broadcasted_iota(jnp.int32, (CHUNK, CHUNK), 0)
    col = lax.broadcasted_iota(jnp.int32, (CHUNK, CHUNK), 1)
    causal = col <= row
    mixed_cols = []
    for gi in range(N_GM_GROUPS):
        wsg = jnp.where(causal, ws_ref[gi], 0.0).astype(BF16)
        bias = bs_ref[:, gi:gi + 1]
        parts = []
        for r0 in range(0, tm, CHUNK):
            parts.append(_dot(wsg, vb[r0:r0 + CHUNK, gi * grp:(gi + 1) * grp]) + bias)
        mixed_cols.append(jnp.concatenate(parts, axis=0))
    mixed = jnp.concatenate(mixed_cols, axis=1)
    y = (u * mixed).astype(BF16)
    o_ref[...] = x + _dot(y, wout_ref[...])

    @pl.when(s == pl.num_programs(1) - 1)
    def _():
        v_ref[...] = v[tm - CHUNK:, :]


def _sc_p_kernel(x_ref, g_ref, win_ref, cw_ref, wout_ref, o_ref, st_ref, ext_ref, *, tm):
    s = pl.program_id(1)
    d = wout_ref.shape[0]

    @pl.when(s == 0)
    def _():
        ext_ref[0:SC_HALO, :] = jnp.zeros((SC_HALO, d), F32)

    x = x_ref[...]
    h = _rms(x, g_ref[...]).astype(BF16)
    z = _dot(h, win_ref[...])
    cx = z[:, d:2 * d] * z[:, 2 * d:]
    ext_ref[SC_HALO:SC_HALO + tm, :] = cx
    conv = cx * cw_ref[SC_WIDTH - 1:SC_WIDTH, :]
    for k in range(SC_WIDTH - 1):
        off = SC_HALO - (SC_WIDTH - 1) + k
        conv = conv + ext_ref[off:off + tm, :] * cw_ref[k:k + 1, :]
    y = (z[:, :d] * conv).astype(BF16)
    o_ref[...] = x + _dot(y, wout_ref[...])

    @pl.when(s == pl.num_programs(1) - 1)
    def _():
        st_ref[...] = ext_ref[SC_HALO + tm - SC_BUF:SC_HALO + tm, :]

    _carry_halo(ext_ref, SC_HALO, tm)


def _prompt_mixer(kern, x, consts, *, state_rows, scratch, name):
    b, s, d = x.shape
    tm = min(TM, s)
    return pl.pallas_call(
        functools.partial(kern, tm=tm),
        grid=(b, s // tm),
        in_specs=[pl.BlockSpec((None, tm, d), lambda i, j: (i, j, 0))]
        + [_const_spec(c.shape) for c in consts],
        out_specs=[
            pl.BlockSpec((None, tm, d), lambda i, j: (i, j, 0)),
            pl.BlockSpec((None, state_rows, d), lambda i, j: (i, 0, 0)),
        ],
        out_shape=[
            jax.ShapeDtypeStruct((b, s, d), F32),
            jax.ShapeDtypeStruct((b, state_rows, d), F32),
        ],
        scratch_shapes=[pltpu.VMEM(shape, F32) for shape in scratch(tm, d)],
        compiler_params=_params(2),
        name=name,
    )(x, *consts)


def _slab(ref, j, d):
    return ref[:, j * d:(j + 1) * d]


def _pool_s_kernel(x_ref, st_ref, g_ref, w_ref, sc_ref, o_ref, nst_ref, d_ref, *, n_new, pos0):
    nb = x_ref.shape[0]
    d = g_ref.shape[1]
    grp = w_ref.shape[1]
    xs = [_slab(x_ref, l, d) for l in range(n_new)]
    hs = [_rms(x, g_ref[...]) for x in xs]

    def ext(j, c0):
        if j < POOL_BUF:
            return st_ref[:, j * d + c0:j * d + c0 + grp]
        return hs[j - POOL_BUF][:, c0:c0 + grp]

    for l in range(n_new):
        for gi, win in enumerate(POOL_WINDOWS):
            c0 = gi * grp
            acc = ext(POOL_BUF + l, c0)
            for k in range(1, win):
                acc = acc + ext(POOL_BUF + l - k, c0)
            cnt = float(min(win, pos0 + l + 1))
            d_ref[l * nb:(l + 1) * nb, c0:c0 + grp] = acc / cnt - hs[l][:, c0:c0 + grp]
    db = d_ref[...].astype(BF16)
    y = jnp.concatenate(
        [_dot(db[:, gi * grp:(gi + 1) * grp], w_ref[gi]) for gi in range(len(POOL_WINDOWS))], axis=1)
    y = y * sc_ref[...]
    for l in range(n_new):
        o_ref[l * nb:(l + 1) * nb, :] = xs[l] + y[l * nb:(l + 1) * nb, :]
    for j in range(POOL_BUF):
        src = j + n_new
        nst_ref[:, j * d:(j + 1) * d] = _slab(st_ref, src, d) if src < POOL_BUF else hs[src - POOL_BUF]


def _conv_s_kernel(x_ref, st_ref, g_ref, win_ref, bin_ref, dw_ref, dwb_ref, lng_ref, lnb_ref,
                   wout_ref, o_ref, nst_ref, g_scr, c_scr, *, n_new):
    nb = st_ref.shape[0]
    dc = wout_ref.shape[0]
    x = x_ref[...]
    h = _rms(x, g_ref[...]).astype(BF16)
    z = _dot(h, win_ref[...]) + bin_ref[...]
    g_scr[...] = z[:, :dc] * _sigmoid(z[:, dc:])

    def ext(j):
        if j < CONV_BUF:
            return _slab(st_ref, j, dc)
        return g_scr[(j - CONV_BUF) * nb:(j - CONV_BUF + 1) * nb, :]

    for l in range(n_new):
        acc = jnp.zeros((nb, dc), F32) + dwb_ref[...]
        for k in range(CONV_WIDTH):
            acc = acc + ext(l + k) * dw_ref[k:k + 1, :]
        c_scr[l * nb:(l + 1) * nb, :] = acc
    c = _silu(_ln(c_scr[...], lng_ref[...], lnb_ref[...]))
    o_ref[...] = x + _dot(c.astype(BF16), wout_ref[...])
    for j in range(CONV_BUF):
        nst_ref[:, j * dc:(j + 1) * dc] = ext(j + n_new)


def _gm_s_kernel(x_ref, g_ref, win_ref, lng_ref, lnb_ref, wrow_ref, brow_ref, wout_ref,
                 o_ref, v_ref, *, n_new, nb):
    dg = wout_ref.shape[0]
    x = x_ref[...]
    h = _rms(x, g_ref[...]).astype(BF16)
    z = _gelu_tanh(_dot(h, win_ref[...]))
    u = z[:, :dg]
    v = _ln(z[:, dg:], lng_ref[...], lnb_ref[...])
    mixed = []
    for t in range(n_new):
        acc = jnp.zeros((nb, dg), F32) + brow_ref[t:t + 1, :]
        for s in range(t + 1):
            acc = acc + wrow_ref[t * n_new + s:t * n_new + s + 1, :] * v[s * nb:(s + 1) * nb, :]
        mixed.append(acc)
    y = (u * jnp.concatenate(mixed, axis=0)).astype(BF16)
    o_ref[...] = x + _dot(y, wout_ref[...])
    for l in range(n_new):
        v_ref[:, l * dg:(l + 1) * dg] = v[l * nb:(l + 1) * nb, :]


def _sc_s_kernel(x_ref, st_ref, g_ref, win_ref, cw_ref, wout_ref, o_ref, nst_ref, *, n_new):
    nb = st_ref.shape[0]
    d = wout_ref.shape[0]
    x = x_ref[...]
    h = _rms(x, g_ref[...]).astype(BF16)
    z = _dot(h, win_ref[...])
    cx = z[:, d:2 * d] * z[:, 2 * d:]

    def ext(j):
        if j < SC_BUF:
            return _slab(st_ref, j, d)
        return cx[(j - SC_BUF) * nb:(j - SC_BUF + 1) * nb, :]

    convs = []
    for l in range(n_new):
        acc = ext(l) * cw_ref[0:1, :]
        for k in range(1, SC_WIDTH):
            acc = acc + ext(l + k) * cw_ref[k:k + 1, :]
        convs.append(acc)
    y = (z[:, :d] * jnp.concatenate(convs, axis=0)).astype(BF16)
    o_ref[...] = x + _dot(y, wout_ref[...])
    for j in range(SC_BUF):
        nst_ref[:, j * d:(j + 1) * d] = ext(j + n_new)


def _single_step(kern, operands, out_shape, scratch, name):
    return pl.pallas_call(
        kern,
        out_shape=out_shape,
        scratch_shapes=scratch,
        compiler_params=pltpu.CompilerParams(vmem_limit_bytes=VMEM_LIMIT),
        name=name,
    )(*operands)


def kernel(x_prompt, x_sample, state_pool, state_conv, state_shortconv, norm_mix, norm_ffn, norm_final, pool_w, pool_scale, conv_w_in, conv_b_in, conv_dw, conv_dw_b, conv_ln_g, conv_ln_b, conv_w_out, gm_w_in, gm_ln_g, gm_ln_b, gm_w_s, gm_b_s, gm_w_out, sc_w_in, sc_conv, sc_w_out, ffn_w_in, ffn_w_out):
    bp, sp, d = x_prompt.shape
    nb, n_new, _ = x_sample.shape
    depth = norm_mix.shape[0]
    assert depth == 4 and pool_w.shape[0] == 1, "one layer of each mixer type"
    assert sp % CHUNK == 0 and n_new < CHUNK and PAST_LEN % CHUNK == 0

    row = lambda a: a.reshape(1, -1)
    bf = lambda a: a.astype(BF16)
    gfin = row(norm_final)
    ffn_in, ffn_out = bf(ffn_w_in), bf(ffn_w_out)

    pool_consts = (row(norm_mix[0]), bf(pool_w[0]), row(pool_scale[0]))
    conv_consts = (row(norm_mix[1]), bf(conv_w_in[0]), row(conv_b_in[0]), conv_dw[0], row(conv_dw_b[0]),
                   row(conv_ln_g[0]), row(conv_ln_b[0]), bf(conv_w_out[0]))
    gm_head = (row(norm_mix[2]), bf(gm_w_in[0]), row(gm_ln_g[0]), row(gm_ln_b[0]))
    gm_out = bf(gm_w_out[0])
    sc_consts = (row(norm_mix[3]), bf(sc_w_in[0]), sc_conv[0], bf(sc_w_out[0]))

    gffn = norm_ffn.reshape(depth, 1, d)

    def ffn(x2d, i):
        return _ffn(x2d, gffn, ffn_in, ffn_out, gfin, layer=i, final=(i == depth - 1))

    def ffn_p(x, i):
        return ffn(x.reshape(bp * sp, d), i).reshape(bp, sp, d)

    x, pool_p = _prompt_mixer(
        functools.partial(_pool_p_kernel, pos0=0), x_prompt, pool_consts, state_rows=POOL_BUF,
        scratch=lambda tm, dd: [(POOL_HALO + tm, dd)], name="pool_p")
    x = ffn_p(x, 0)
    x, conv_p = _conv_p(x, conv_consts)
    x = ffn_p(x, 1)
    x, v_p = _prompt_mixer(
        _gm_p_kernel, x, gm_head + (gm_w_s[0], gm_b_s[0].T, gm_out), state_rows=CHUNK,
        scratch=lambda tm, dd: [], name="gm_p")
    x = ffn_p(x, 2)
    x, sc_p = _prompt_mixer(
        _sc_p_kernel, x, sc_consts, state_rows=SC_BUF,
        scratch=lambda tm, dd: [(SC_HALO + tm, dd)], name="sc_p")
    y_prompt = ffn_p(x, 3)

    m = nb * n_new
    xs2 = x_sample.reshape(nb, n_new * d)
    tm_shape = jax.ShapeDtypeStruct((m, d), F32)
    xs, pool_s = _single_step(
        functools.partial(_pool_s_kernel, n_new=n_new, pos0=PAST_LEN),
        (xs2, state_pool[0].reshape(nb, POOL_BUF * d)) + pool_consts,
        [tm_shape, jax.ShapeDtypeStruct((nb, POOL_BUF * d), F32)],
        [pltpu.VMEM((m, d), F32)], "pool_s")
    xs = ffn(xs, 0)
    xs, conv_s = _single_step(
        functools.partial(_conv_s_kernel, n_new=n_new),
        (xs, state_conv[0].reshape(nb, CONV_BUF * d)) + conv_consts,
        [tm_shape, jax.ShapeDtypeStruct((nb, CONV_BUF * d), F32)],
        [pltpu.VMEM((m, d), F32), pltpu.VMEM((m, d), F32)], "conv_s")
    xs = ffn(xs, 1)
    grp = d // N_GM_GROUPS
    wrow = jnp.repeat(jnp.transpose(gm_w_s[0][:, :n_new, :n_new], (1, 2, 0)).reshape(n_new * n_new, N_GM_GROUPS), grp, axis=1)
    brow = jnp.repeat(gm_b_s[0][:, :n_new].T, grp, axis=1)
    xs, v_s = _single_step(
        functools.partial(_gm_s_kernel, n_new=n_new, nb=nb),
        (xs,) + gm_head + (wrow, brow, gm_out),
        [tm_shape, jax.ShapeDtypeStruct((nb, n_new * d), F32)],
        [], "gm_s")
    xs = ffn(xs, 2)
    xs, sc_s = _single_step(
        functools.partial(_sc_s_kernel, n_new=n_new),
        (xs, state_shortconv[0].reshape(nb, SC_BUF * d)) + sc_consts,
        [tm_shape, jax.ShapeDtypeStruct((nb, SC_BUF * d), F32)],
        [], "sc_s")
    ys = ffn(xs, 3)
    y_sample = jnp.transpose(ys.reshape(n_new, nb, d), (1, 0, 2))

    return (y_prompt, y_sample,
            pool_p[None], pool_s.reshape(1, nb, POOL_BUF, d),
            conv_p[None], conv_s.reshape(1, nb, CONV_BUF, d),
            v_p[None], v_s.reshape(1, nb, n_new, d),
            sc_p[None], sc_s.reshape(1, nb, SC_BUF, d))
```

```python
import functools

import jax
import jax.numpy as jnp
from jax import lax
from jax.experimental import pallas as pl
from jax.experimental.pallas import tpu as pltpu

F32 = jnp.float32
BF16 = jnp.bfloat16

EPS = 1e-6
POOL_WINDOWS = (2, 4, 8, 16)
POOL_BUF = max(POOL_WINDOWS) - 1
CONV_WIDTH = 31
CONV_BUF = CONV_WIDTH - 1
CHUNK = 128
N_GM_GROUPS = 4
SC_WIDTH = 3
SC_BUF = SC_WIDTH - 1
PAST_LEN = 16384

V7X_SUBLANES = 8
V7X_LANES = 128
V7X_MXU_DIM = 256
V7X_VMEM_BYTES = 64 * 1024 * 1024
VMEM_LIMIT = V7X_VMEM_BYTES - 8 * 1024 * 1024

TM = 1024
POOL_HALO = 16
CONV_HALO = 32
SC_HALO = 8
CONV_UNROLL = 16
CONV_GROUP = 8
CONV_HEAD = CONV_WIDTH % CONV_GROUP


def _params(n_grid):
    return pltpu.CompilerParams(
        dimension_semantics=("arbitrary",) * n_grid, vmem_limit_bytes=VMEM_LIMIT)


def _const_spec(shape):
    nd = len(shape)
    return pl.BlockSpec(shape, lambda *_: (0,) * nd, pipeline_mode=pl.Buffered(1))


def _rms(x, g):
    y = x * lax.rsqrt(jnp.mean(x * x, axis=-1, keepdims=True) + EPS)
    return y * g


def _ln(x, g, b):
    mu = jnp.mean(x, axis=-1, keepdims=True)
    xc = x - mu
    y = xc * lax.rsqrt(jnp.mean(xc * xc, axis=-1, keepdims=True) + EPS)
    return y * g + b


def _sigmoid(x):
    return 0.5 * (1.0 + jnp.tanh(0.5 * x))


def _silu(x):
    return x * _sigmoid(x)


def _gelu_tanh(x):
    c = 0.7978845608028654
    return x * (0.5 * (1.0 + jnp.tanh(c * (x + 0.044715 * (x * x * x)))))


def _dot(a, b):
    return jnp.dot(a, b, preferred_element_type=F32)


def _ffn_chunks(d_ff, n_chunks):
    tiles = d_ff // V7X_MXU_DIM
    assert tiles * V7X_MXU_DIM == d_ff
    edges = [V7X_MXU_DIM * ((tiles * i + n_chunks - 1) // n_chunks) for i in range(n_chunks + 1)]
    return list(zip(edges[:-1], edges[1:]))


def _ffn_kernel(x_ref, g_ref, win_ref, wout_ref, gf_ref, o_ref, *, d_ff, chunks, final):
    x = x_ref[...]
    h = _rms(x, g_ref[...]).astype(BF16)
    acc = x
    for c0, c1 in chunks:
        gate = _dot(h, win_ref[:, c0:c1])
        up = _dot(h, win_ref[:, d_ff + c0:d_ff + c1])
        act = (_silu(gate) * up).astype(BF16)
        acc = acc + _dot(act, wout_ref[c0:c1, :])
    if final:
        acc = _rms(acc, gf_ref[...])
    o_ref[...] = acc


def _layer_spec(shape, layer):
    nd = len(shape)
    return pl.BlockSpec((None,) + tuple(shape[1:]), lambda *_: (layer,) + (0,) * (nd - 1),
                        pipeline_mode=pl.Buffered(1))


def _ffn(x, g, w_in, w_out, g_final, *, layer, final):
    m, d = x.shape
    d_ff = w_out.shape[1]
    tm = min(TM, m)
    kern = functools.partial(_ffn_kernel, d_ff=d_ff, chunks=_ffn_chunks(d_ff, 2), final=final)
    return pl.pallas_call(
        kern,
        grid=(m // tm,),
        in_specs=[
            pl.BlockSpec((tm, d), lambda i: (i, 0)),
            _layer_spec(g.shape, layer),
            _layer_spec(w_in.shape, layer),
            _layer_spec(w_out.shape, layer),
            _const_spec((1, d)),
        ],
        out_specs=pl.BlockSpec((tm, d), lambda i: (i, 0)),
        out_shape=jax.ShapeDtypeStruct((m, d), F32),
        compiler_params=_params(1),
        name="ffn_final" if final else "ffn",
    )(x, g, w_in, w_out, g_final)


def _carry_halo(ext_ref, halo, tm):
    ext_ref[0:halo, :] = ext_ref[tm:tm + halo, :]


def _pool_p_kernel(x_ref, g_ref, w_ref, sc_ref, o_ref, st_ref, ext_ref, *, tm, pos0):
    s = pl.program_id(1)
    grp = w_ref.shape[1]

    @pl.when(s == 0)
    def _():
        ext_ref[0:POOL_HALO, :] = jnp.zeros((POOL_HALO, ext_ref.shape[1]), F32)

    x = x_ref[...]
    h = _rms(x, g_ref[...])
    ext_ref[POOL_HALO:POOL_HALO + tm, :] = h
    pos = pos0 + s * tm + lax.broadcasted_iota(jnp.int32, (tm, 1), 0)
    ys = []
    for gi, win in enumerate(POOL_WINDOWS):
        c0 = gi * grp
        hg = h[:, c0:c0 + grp]
        acc = hg
        for k in range(1, win):
            acc = acc + ext_ref[POOL_HALO - k:POOL_HALO - k + tm, c0:c0 + grp]
        cnt = jnp.minimum(win, pos + 1).astype(F32)
        d = acc / cnt - hg
        ys.append(_dot(d.astype(BF16), w_ref[gi]))
    y = jnp.concatenate(ys, axis=1)
    o_ref[...] = x + y * sc_ref[...]

    @pl.when(s == pl.num_programs(1) - 1)
    def _():
        st_ref[...] = ext_ref[POOL_HALO + tm - POOL_BUF:POOL_HALO + tm, :]

    _carry_halo(ext_ref, POOL_HALO, tm)


def _conv_p_kernel(x_ref, g_ref, win_ref, bin_ref, dw_ref, dwb_ref, lng_ref, lnb_ref, wout_ref,
                   o_ref, st_ref, glu_ref, ext_ref, c_ref, wb_ref, *, nb, ts):
    s = pl.program_id(0)
    d = x_ref.shape[2]
    n_lt, _, lane = glu_ref.shape
    pitch = glu_ref.shape[1] // nb

    @pl.when(s == 0)
    def _():
        ext_ref[:, 0:CONV_HALO * nb, :] = jnp.zeros((n_lt, CONV_HALO * nb, lane), F32)
        for j in range(n_lt):
            for k in range(CONV_WIDTH):
                wb_ref[j, k * nb:(k + 1) * nb, :] = jnp.broadcast_to(
                    dw_ref[k:k + 1, j * lane:(j + 1) * lane], (nb, lane))

    x = x_ref[...].reshape(nb * ts, d)
    h = _rms(x, g_ref[...]).astype(BF16)
    z = _dot(h, win_ref[...]) + bin_ref[...]
    dc = n_lt * lane
    glu = z[:, :dc] * _sigmoid(z[:, dc:])
    base = CONV_HALO - CONV_BUF
    for j in range(n_lt):
        lanes = slice(j * lane, (j + 1) * lane)
        for b in range(nb):
            glu_ref[j, b * pitch:b * pitch + ts, :] = glu[b * ts:(b + 1) * ts, lanes]
        for t in range(ts):
            ext_ref[j, (CONV_HALO + t) * nb:(CONV_HALO + t + 1) * nb, :] = (
                glu_ref[j, pl.ds(t, nb, stride=pitch), :])

        bias = jnp.broadcast_to(dwb_ref[:, lanes], (nb, lane))

        def taps(acc, t_first, k_first, n_taps, j=j):
            r0 = (t_first + base + k_first) * nb
            rows = [ext_ref[j, pl.ds(pl.multiple_of(r0 + m * nb, nb), nb), :]
                    for m in range(CONV_UNROLL + n_taps - 1)]
            for kk in range(n_taps):
                w = wb_ref[j, pl.ds(pl.multiple_of((k_first + kk) * nb, nb), nb), :]
                acc = [a + rows[u + kk] * w for u, a in enumerate(acc)]
            return acc

        def body(i, carry, bias=bias, taps=taps, j=j):
            t0 = i * CONV_UNROLL
            acc = taps([bias] * CONV_UNROLL, t0, 0, CONV_HEAD)
            acc = lax.fori_loop(
                0, (CONV_WIDTH - CONV_HEAD) // CONV_GROUP,
                lambda g, a: tuple(taps(list(a), t0, CONV_HEAD + g * CONV_GROUP, CONV_GROUP)),
                tuple(acc))
            for u in range(CONV_UNROLL):
                c_ref[j, pl.ds(pl.multiple_of((t0 + u) * nb, nb), nb), :] = acc[u]
            return carry

        lax.fori_loop(0, ts // CONV_UNROLL, body, 0)

    c = jnp.concatenate(
        [jnp.concatenate([c_ref[j, pl.ds(b, ts, stride=nb), :] for b in range(nb)], axis=0)
         for j in range(n_lt)], axis=1)
    c = _silu(_ln(c, lng_ref[...], lnb_ref[...]))
    o_ref[...] = (x + _dot(c.astype(BF16), wout_ref[...])).reshape(nb, ts, d)

    @pl.when(s == pl.num_programs(0) - 1)
    def _():
        for b in range(nb):
            for j in range(n_lt):
                st_ref[b, :, j * lane:(j + 1) * lane] = (
                    glu_ref[j, b * pitch + ts - CONV_BUF:b * pitch + ts, :])

    ext_ref[:, 0:CONV_HALO * nb, :] = ext_ref[:, ts * nb:(ts + CONV_HALO) * nb, :]


def _conv_p(x, consts):
    nb, s, d = x.shape
    dc = consts[-1].shape[0]
    assert nb == V7X_SUBLANES, "the time-major copy puts one sequence on each sublane"
    ts = TM // nb
    assert s % ts == 0 and ts >= CONV_HALO
    return pl.pallas_call(
        functools.partial(_conv_p_kernel, nb=nb, ts=ts),
        grid=(s // ts,),
        in_specs=[pl.BlockSpec((nb, ts, d), lambda j: (0, j, 0))] + [_const_spec(c.shape) for c in consts],
        out_specs=[
            pl.BlockSpec((nb, ts, d), lambda j: (0, j, 0)),
            pl.BlockSpec((nb, CONV_BUF, dc), lambda j: (0, 0, 0)),
        ],
        out_shape=[
            jax.ShapeDtypeStruct((nb, s, d), F32),
            jax.ShapeDtypeStruct((nb, CONV_BUF, dc), F32),
        ],
        scratch_shapes=[
            pltpu.VMEM((dc // V7X_LANES, nb * (ts + V7X_SUBLANES), V7X_LANES), F32),
            pltpu.VMEM((dc // V7X_LANES, (CONV_HALO + ts) * nb, V7X_LANES), F32),
            pltpu.VMEM((dc // V7X_LANES, ts * nb, V7X_LANES), F32),
            pltpu.VMEM((dc // V7X_LANES, CONV_WIDTH * nb, V7X_LANES), F32),
        ],
        compiler_params=_params(1),
        name="conv_p",
    )(x, *consts)


def _gm_p_kernel(x_ref, g_ref, win_ref, lng_ref, lnb_ref, ws_ref, bs_ref, wout_ref,
                 o_ref, v_ref, *, tm):
    s = pl.program_id(1)
    dg = wout_ref.shape[0]
    grp = dg // N_GM_GROUPS
    row = lax.broadcasted_iota(jnp.int32, (CHUNK, CHUNK), 0)
    col = lax.broadcasted_iota(jnp.int32, (CHUNK, CHUNK), 1)
    causal = col <= row
    x = x_ref[...]
    h = _rms(x, g_ref[...]).astype(BF16)
    z = _gelu_tanh(_dot(h, win_ref[...]))
    u = z[:, :dg]
    v = _ln(z[:, dg:], lng_ref[...], lnb_ref[...])
    vb = v.astype(BF16)
    mixed_cols = []
    for gi in range(N_GM_GROUPS):
        wsg = jnp.where(causal, ws_ref[gi], 0.0).astype(BF16)
        bias = bs_ref[:, gi:gi + 1]
        parts = [_dot(wsg, vb[c0:c0 + CHUNK, gi * grp:(gi + 1) * grp]) + bias
                 for c0 in range(0, tm, CHUNK)]
        mixed_cols.append(jnp.concatenate(parts, axis=0))
    mixed = jnp.concatenate(mixed_cols, axis=1)
    y = (u * mixed).astype(BF16)
    o_ref[...] = x + _dot(y, wout_ref[...])

    @pl.when(s == pl.num_programs(1) - 1)
    def _():
        v_ref[...] = v[tm - CHUNK:, :]


def _sc_p_kernel(x_ref, g_ref, win_ref, cw_ref, wout_ref, o_ref, st_ref, ext_ref, *, tm):
    s = pl.program_id(1)
    d = wout_ref.shape[0]

    @pl.when(s == 0)
    def _():
        ext_ref[0:SC_HALO, :] = jnp.zeros((SC_HALO, d), F32)

    x = x_ref[...]
    h = _rms(x, g_ref[...]).astype(BF16)
    z = _dot(h, win_ref[...])
    cx = z[:, d:2 * d] * z[:, 2 * d:]
    ext_ref[SC_HALO:SC_HALO + tm, :] = cx
    conv = cx * cw_ref[SC_WIDTH - 1:SC_WIDTH, :]
    for k in range(SC_WIDTH - 1):
        off = SC_HALO - (SC_WIDTH - 1) + k
        conv = conv + ext_ref[off:off + tm, :] * cw_ref[k:k + 1, :]
    y = (z[:, :d] * conv).astype(BF16)
    o_ref[...] = x + _dot(y, wout_ref[...])

    @pl.when(s == pl.num_programs(1) - 1)
    def _():
        st_ref[...] = ext_ref[SC_HALO + tm - SC_BUF:SC_HALO + tm, :]

    _carry_halo(ext_ref, SC_HALO, tm)


def _prompt_mixer(kern, x, consts, *, state_rows, scratch, name):
    b, s, d = x.shape
    tm = min(TM, s)
    return pl.pallas_call(
        functools.partial(kern, tm=tm),
        grid=(b, s // tm),
        in_specs=[pl.BlockSpec((None, tm, d), lambda i, j: (i, j, 0))]
        + [_const_spec(c.shape) for c in consts],
        out_specs=[
            pl.BlockSpec((None, tm, d), lambda i, j: (i, j, 0)),
            pl.BlockSpec((None, state_rows, d), lambda i, j: (i, 0, 0)),
        ],
        out_shape=[
            jax.ShapeDtypeStruct((b, s, d), F32),
            jax.ShapeDtypeStruct((b, state_rows, d), F32),
        ],
        scratch_shapes=[pltpu.VMEM(shape, F32) for shape in scratch(tm, d)],
        compiler_params=_params(2),
        name=name,
    )(x, *consts)


def _slab(ref, j, d):
    return ref[:, j * d:(j + 1) * d]


def _pool_s_kernel(x_ref, st_ref, g_ref, w_ref, sc_ref, o_ref, nst_ref, d_ref, *, n_new, pos0):
    nb = x_ref.shape[0]
    d = g_ref.shape[1]
    grp = w_ref.shape[1]
    xs = [_slab(x_ref, l, d) for l in range(n_new)]
    hs = [_rms(x, g_ref[...]) for x in xs]

    def ext(j, c0):
        if j < POOL_BUF:
            return st_ref[:, j * d + c0:j * d + c0 + grp]
        return hs[j - POOL_BUF][:, c0:c0 + grp]

    for l in range(n_new):
        for gi, win in enumerate(POOL_WINDOWS):
            c0 = gi * grp
            acc = ext(POOL_BUF + l, c0)
            for k in range(1, win):
                acc = acc + ext(POOL_BUF + l - k, c0)
            cnt = float(min(win, pos0 + l + 1))
            d_ref[l * nb:(l + 1) * nb, c0:c0 + grp] = acc / cnt - hs[l][:, c0:c0 + grp]
    db = d_ref[...].astype(BF16)
    y = jnp.concatenate(
        [_dot(db[:, gi * grp:(gi + 1) * grp], w_ref[gi]) for gi in range(len(POOL_WINDOWS))], axis=1)
    y = y * sc_ref[...]
    for l in range(n_new):
        o_ref[l * nb:(l + 1) * nb, :] = xs[l] + y[l * nb:(l + 1) * nb, :]
    for j in range(POOL_BUF):
        src = j + n_new
        nst_ref[:, j * d:(j + 1) * d] = _slab(st_ref, src, d) if src < POOL_BUF else hs[src - POOL_BUF]


def _conv_s_kernel(x_ref, st_ref, g_ref, win_ref, bin_ref, dw_ref, dwb_ref, lng_ref, lnb_ref,
                   wout_ref, o_ref, nst_ref, g_scr, c_scr, *, n_new):
    nb = st_ref.shape[0]
    dc = wout_ref.shape[0]
    x = x_ref[...]
    h = _rms(x, g_ref[...]).astype(BF16)
    z = _dot(h, win_ref[...]) + bin_ref[...]
    g_scr[...] = z[:, :dc] * _sigmoid(z[:, dc:])

    def ext(j):
        if j < CONV_BUF:
            return _slab(st_ref, j, dc)
        return g_scr[(j - CONV_BUF) * nb:(j - CONV_BUF + 1) * nb, :]

    for l in range(n_new):
        acc = jnp.zeros((nb, dc), F32) + dwb_ref[...]
        for k in range(CONV_WIDTH):
            acc = acc + ext(l + k) * dw_ref[k:k + 1, :]
        c_scr[l * nb:(l + 1) * nb, :] = acc
    c = _silu(_ln(c_scr[...], lng_ref[...], lnb_ref[...]))
    o_ref[...] = x + _dot(c.astype(BF16), wout_ref[...])
    for j in range(CONV_BUF):
        nst_ref[:, j * dc:(j + 1) * dc] = ext(j + n_new)


def _gm_s_kernel(x_ref, g_ref, win_ref, lng_ref, lnb_ref, wrow_ref, brow_ref, wout_ref,
                 o_ref, v_ref, *, n_new, nb):
    dg = wout_ref.shape[0]
    x = x_ref[...]
    h = _rms(x, g_ref[...]).astype(BF16)
    z = _gelu_tanh(_dot(h, win_ref[...]))
    u = z[:, :dg]
    v = _ln(z[:, dg:], lng_ref[...], lnb_ref[...])
    mixed = []
    for t in range(n_new):
        acc = jnp.zeros((nb, dg), F32) + brow_ref[t:t + 1, :]
        for s in range(t + 1):
            acc = acc + wrow_ref[t * n_new + s:t * n_new + s + 1, :] * v[s * nb:(s + 1) * nb, :]
        mixed.append(acc)
    y = (u * jnp.concatenate(mixed, axis=0)).astype(BF16)
    o_ref[...] = x + _dot(y, wout_ref[...])
    for l in range(n_new):
        v_ref[:, l * dg:(l + 1) * dg] = v[l * nb:(l + 1) * nb, :]


def _sc_s_kernel(x_ref, st_ref, g_ref, win_ref, cw_ref, wout_ref, o_ref, nst_ref, *, n_new):
    nb = st_ref.shape[0]
    d = wout_ref.shape[0]
    x = x_ref[...]
    h = _rms(x, g_ref[...]).astype(BF16)
    z = _dot(h, win_ref[...])
    cx = z[:, d:2 * d] * z[:, 2 * d:]

    def ext(j):
        if j < SC_BUF:
            return _slab(st_ref, j, d)
        return cx[(j - SC_BUF) * nb:(j - SC_BUF + 1) * nb, :]

    convs = []
    for l in range(n_new):
        acc = ext(l) * cw_ref[0:1, :]
        for k in range(1, SC_WIDTH):
            acc = acc + ext(l + k) * cw_ref[k:k + 1, :]
        convs.append(acc)
    y = (z[:, :d] * jnp.concatenate(convs, axis=0)).astype(BF16)
    o_ref[...] = x + _dot(y, wout_ref[...])
    for j in range(SC_BUF):
        nst_ref[:, j * d:(j + 1) * d] = ext(j + n_new)


def _single_step(kern, operands, out_shape, scratch, name):
    return pl.pallas_call(
        kern,
        out_shape=out_shape,
        scratch_shapes=scratch,
        compiler_params=pltpu.CompilerParams(vmem_limit_bytes=VMEM_LIMIT),
        name=name,
    )(*operands)


def kernel(x_prompt, x_sample, state_pool, state_conv, state_shortconv, norm_mix, norm_ffn, norm_final, pool_w, pool_scale, conv_w_in, conv_b_in, conv_dw, conv_dw_b, conv_ln_g, conv_ln_b, conv_w_out, gm_w_in, gm_ln_g, gm_ln_b, gm_w_s, gm_b_s, gm_w_out, sc_w_in, sc_conv, sc_w_out, ffn_w_in, ffn_w_out):
    bp, sp, d = x_prompt.shape
    nb, n_new, _ = x_sample.shape
    depth = norm_mix.shape[0]
    assert depth == 4 and pool_w.shape[0] == 1, "one layer of each mixer type"
    assert sp % CHUNK == 0 and n_new < CHUNK and PAST_LEN % CHUNK == 0

    row = lambda a: a.reshape(1, -1)
    bf = lambda a: a.astype(BF16)
    gfin = row(norm_final)
    ffn_in, ffn_out = bf(ffn_w_in), bf(ffn_w_out)

    pool_consts = (row(norm_mix[0]), bf(pool_w[0]), row(pool_scale[0]))
    conv_consts = (row(norm_mix[1]), bf(conv_w_in[0]), row(conv_b_in[0]), conv_dw[0], row(conv_dw_b[0]),
                   row(conv_ln_g[0]), row(conv_ln_b[0]), bf(conv_w_out[0]))
    gm_head = (row(norm_mix[2]), bf(gm_w_in[0]), row(gm_ln_g[0]), row(gm_ln_b[0]))
    gm_out = bf(gm_w_out[0])
    sc_consts = (row(norm_mix[3]), bf(sc_w_in[0]), sc_conv[0], bf(sc_w_out[0]))

    gffn = norm_ffn.reshape(depth, 1, d)

    def ffn(x2d, i):
        return _ffn(x2d, gffn, ffn_in, ffn_out, gfin, layer=i, final=(i == depth - 1))

    def ffn_p(x, i):
        return ffn(x.reshape(bp * sp, d), i).reshape(bp, sp, d)

    x, pool_p = _prompt_mixer(
        functools.partial(_pool_p_kernel, pos0=0), x_prompt, pool_consts, state_rows=POOL_BUF,
        scratch=lambda tm, dd: [(POOL_HALO + tm, dd)], name="pool_p")
    x = ffn_p(x, 0)
    x, conv_p = _conv_p(x, conv_consts)
    x = ffn_p(x, 1)
    x, v_p = _prompt_mixer(
        _gm_p_kernel, x, gm_head + (gm_w_s[0], gm_b_s[0].T, gm_out), state_rows=CHUNK,
        scratch=lambda tm, dd: [], name="gm_p")
    x = ffn_p(x, 2)
    x, sc_p = _prompt_mixer(
        _sc_p_kernel, x, sc_consts, state_rows=SC_BUF,
        scratch=lambda tm, dd: [(SC_HALO + tm, dd)], name="sc_p")
    y_prompt = ffn_p(x, 3)

    m = nb * n_new
    xs2 = x_sample.reshape(nb, n_new * d)
    tm_shape = jax.ShapeDtypeStruct((m, d), F32)
    xs, pool_s = _single_step(
        functools.partial(_pool_s_kernel, n_new=n_new, pos0=PAST_LEN),
        (xs2, state_pool[0].reshape(nb, POOL_BUF * d)) + pool_consts,
        [tm_shape, jax.ShapeDtypeStruct((nb, POOL_BUF * d), F32)],
        [pltpu.VMEM((m, d), F32)], "pool_s")
    xs = ffn(xs, 0)
    xs, conv_s = _single_step(
        functools.partial(_conv_s_kernel, n_new=n_new),
        (xs, state_conv[0].reshape(nb, CONV_BUF * d)) + conv_consts,
        [tm_shape, jax.ShapeDtypeStruct((nb, CONV_BUF * d), F32)],
        [pltpu.VMEM((m, d), F32), pltpu.VMEM((m, d), F32)], "conv_s")
    xs = ffn(xs, 1)
    grp = d // N_GM_GROUPS
    wrow = jnp.repeat(jnp.transpose(gm_w_s[0][:, :n_new, :n_new], (1, 2, 0)).reshape(n_new * n_new, N_GM_GROUPS), grp, axis=1)
    brow = jnp.repeat(gm_b_s[0][:, :n_new].T, grp, axis=1)
    xs, v_s = _single_step(
        functools.partial(_gm_s_kernel, n_new=n_new, nb=nb),
        (xs,) + gm_head + (wrow, brow, gm_out),
        [tm_shape, jax.ShapeDtypeStruct((nb, n_new * d), F32)],
        [], "gm_s")
    xs = ffn(xs, 2)
    xs, sc_s = _single_step(
        functools.partial(_sc_s_kernel, n_new=n_new),
        (xs, state_shortconv[0].reshape(nb, SC_BUF * d)) + sc_consts,
        [tm_shape, jax.ShapeDtypeStruct((nb, SC_BUF * d), F32)],
        [], "sc_s")
    ys = ffn(xs, 3)
    y_sample = jnp.transpose(ys.reshape(n_new, nb, d), (1, 0, 2))

    return (y_prompt, y_sample,
            pool_p[None], pool_s.reshape(1, nb, POOL_BUF, d),
            conv_p[None], conv_s.reshape(1, nb, CONV_BUF, d),
            v_p[None], v_s.reshape(1, nb, n_new, d),
            sc_p[None], sc_s.reshape(1, nb, SC_BUF, d))
```

```python
import functools

import jax
import jax.numpy as jnp
from jax import lax
from jax.experimental import pallas as pl
from jax.experimental.pallas import tpu as pltpu

F32 = jnp.float32
BF16 = jnp.bfloat16

EPS = 1e-6
POOL_WINDOWS = (2, 4, 8, 16)
POOL_BUF = max(POOL_WINDOWS) - 1
CONV_WIDTH = 31
CONV_BUF = CONV_WIDTH - 1
CHUNK = 128
N_GM_GROUPS = 4
SC_WIDTH = 3
SC_BUF = SC_WIDTH - 1
PAST_LEN = 16384

V7X_SUBLANES = 8
V7X_BF16_SUBLANES = 16
V7X_LANES = 128
V7X_MXU_DIM = 256
V7X_VMEM_BYTES = 64 * 1024 * 1024
VMEM_LIMIT = V7X_VMEM_BYTES - 4 * 1024 * 1024

TM = 1024
POOL_HALO = 16
CONV_HALO = 32
SC_HALO = 8
CONV_UNROLL = 16
CONV_GROUP = 8
CONV_HEAD = CONV_WIDTH % CONV_GROUP


def _params(n_grid):
    return pltpu.CompilerParams(
        dimension_semantics=("arbitrary",) * n_grid, vmem_limit_bytes=VMEM_LIMIT)


def _const_spec(shape):
    nd = len(shape)
    return pl.BlockSpec(shape, lambda *_: (0,) * nd, pipeline_mode=pl.Buffered(1))


def _rms(x, g):
    y = x * lax.rsqrt(jnp.mean(x * x, axis=-1, keepdims=True) + EPS)
    return y * g


def _ln(x, g, b):
    mu = jnp.mean(x, axis=-1, keepdims=True)
    xc = x - mu
    y = xc * lax.rsqrt(jnp.mean(xc * xc, axis=-1, keepdims=True) + EPS)
    return y * g + b


def _sigmoid(x):
    return 0.5 * (1.0 + jnp.tanh(0.5 * x))


def _silu(x):
    return x * _sigmoid(x)


def _gelu_tanh(x):
    c = 0.7978845608028654
    return x * (0.5 * (1.0 + jnp.tanh(c * (x + 0.044715 * (x * x * x)))))


def _dot(a, b):
    return jnp.dot(a, b, preferred_element_type=F32)


def _ffn_chunks(d_ff, n_chunks):
    tiles = d_ff // V7X_MXU_DIM
    assert tiles * V7X_MXU_DIM == d_ff
    edges = [V7X_MXU_DIM * ((tiles * i + n_chunks - 1) // n_chunks) for i in range(n_chunks + 1)]
    return list(zip(edges[:-1], edges[1:]))


def _ffn_kernel(*refs, d_ff, chunks, final, n_cast, n_prompt_steps):
    xp_ref, xs_ref, g_ref, win_ref, wout_ref, gf_ref = refs[:6]
    cast_src = refs[6:6 + n_cast]
    op_ref, os_ref = refs[6 + n_cast:8 + n_cast]
    cast_dst = refs[8 + n_cast:]
    i = pl.program_id(0)

    def rows(x_ref, o_ref):
        x = x_ref[...]
        h = _rms(x, g_ref[...]).astype(BF16)
        acc = x
        for c0, c1 in chunks:
            gate = _dot(h, win_ref[:, c0:c1])
            up = _dot(h, win_ref[:, d_ff + c0:d_ff + c1])
            act = (_silu(gate) * up).astype(BF16)
            acc = acc + _dot(act, wout_ref[c0:c1, :])
        if final:
            acc = _rms(acc, gf_ref[...])
        o_ref[...] = acc

    @pl.when(i < n_prompt_steps)
    def _():
        rows(xp_ref, op_ref)
        for src, dst in zip(cast_src, cast_dst):
            dst[...] = src[...].astype(BF16)

    @pl.when(i == n_prompt_steps)
    def _():
        rows(xs_ref, os_ref)


def _layer_spec(shape, layer):
    nd = len(shape)
    return pl.BlockSpec((None,) + tuple(shape[1:]), lambda *_: (layer,) + (0,) * (nd - 1),
                        pipeline_mode=pl.Buffered(1))


def _ffn(xp, xs, g, w_in, w_out, g_final, casts, *, layer, final):
    mp, d = xp.shape
    ms = xs.shape[0]
    d_ff = w_out.shape[0]
    assert mp % TM == 0
    n_steps = mp // TM
    last = n_steps - 1
    kern = functools.partial(_ffn_kernel, d_ff=d_ff, chunks=_ffn_chunks(d_ff, 2), final=final,
                             n_cast=len(casts), n_prompt_steps=n_steps)
    cast_in, cast_out, cast_shape = [], [], []
    for arr, idx in casts:
        _, r, c = arr.shape
        assert r % (n_steps * V7X_BF16_SUBLANES) == 0
        cast_in.append(pl.BlockSpec((None, r // n_steps, c),
                                    lambda i, idx=idx: (idx, jnp.minimum(i, last), 0)))
        cast_out.append(pl.BlockSpec((r // n_steps, c), lambda i: (jnp.minimum(i, last), 0)))
        cast_shape.append(jax.ShapeDtypeStruct((r, c), BF16))
    return pl.pallas_call(
        kern,
        grid=(n_steps + 1,),
        in_specs=[
            pl.BlockSpec((TM, d), lambda i: (jnp.minimum(i, last), 0)),
            _const_spec((ms, d)),
            _layer_spec(g.shape, layer),
            _const_spec(w_in.shape),
            _const_spec(w_out.shape),
            _const_spec((1, d)),
        ] + cast_in,
        out_specs=[
            pl.BlockSpec((TM, d), lambda i: (jnp.minimum(i, last), 0)),
            pl.BlockSpec((ms, d), lambda i: (0, 0)),
        ] + cast_out,
        out_shape=[jax.ShapeDtypeStruct((mp, d), F32), jax.ShapeDtypeStruct((ms, d), F32)] + cast_shape,
        compiler_params=_params(1),
        name="ffn_final" if final else "ffn",
    )(xp, xs, g, w_in, w_out, g_final, *[arr for arr, _ in casts])


def _carry_halo(ext_ref, halo, tm):
    ext_ref[0:halo, :] = ext_ref[tm:tm + halo, :]


def _pool_p_kernel(x_ref, g_ref, w_ref, sc_ref, o_ref, st_ref, ext_ref, *, tm, pos0):
    s = pl.program_id(1)
    grp = w_ref.shape[1]

    @pl.when(s == 0)
    def _():
        ext_ref[0:POOL_HALO, :] = jnp.zeros((POOL_HALO, ext_ref.shape[1]), F32)

    x = x_ref[...]
    h = _rms(x, g_ref[...])
    ext_ref[POOL_HALO:POOL_HALO + tm, :] = h
    pos = pos0 + s * tm + lax.broadcasted_iota(jnp.int32, (tm, 1), 0)
    ys = []
    for gi, win in enumerate(POOL_WINDOWS):
        c0 = gi * grp
        hg = h[:, c0:c0 + grp]
        acc = hg
        for k in range(1, win):
            acc = acc + ext_ref[POOL_HALO - k:POOL_HALO - k + tm, c0:c0 + grp]
        cnt = jnp.minimum(win, pos + 1).astype(F32)
        d = acc / cnt - hg
        ys.append(_dot(d.astype(BF16), w_ref[gi]))
    y = jnp.concatenate(ys, axis=1)
    o_ref[...] = x + y * sc_ref[...]

    @pl.when(s == pl.num_programs(1) - 1)
    def _():
        st_ref[...] = ext_ref[POOL_HALO + tm - POOL_BUF:POOL_HALO + tm, :]

    _carry_halo(ext_ref, POOL_HALO, tm)


def _conv_p_kernel(x_ref, g_ref, win_ref, bin_ref, dw_ref, dwb_ref, lng_ref, lnb_ref, wout_ref,
                   o_ref, st_ref, glu_ref, ext_ref, c_ref, wb_ref, *, nb, ts):
    s = pl.program_id(0)
    d = x_ref.shape[2]
    n_lt, _, lane = glu_ref.shape
    pitch = glu_ref.shape[1] // nb

    @pl.when(s == 0)
    def _():
        ext_ref[:, 0:CONV_HALO * nb, :] = jnp.zeros((n_lt, CONV_HALO * nb, lane), F32)
        for j in range(n_lt):
            for k in range(CONV_WIDTH):
                wb_ref[j, k * nb:(k + 1) * nb, :] = jnp.broadcast_to(
                    dw_ref[k:k + 1, j * lane:(j + 1) * lane], (nb, lane))

    x = x_ref[...].reshape(nb * ts, d)
    h = _rms(x, g_ref[...]).astype(BF16)
    z = _dot(h, win_ref[...]) + bin_ref[...]
    dc = n_lt * lane
    glu = z[:, :dc] * _sigmoid(z[:, dc:])
    base = CONV_HALO - CONV_BUF
    for j in range(n_lt):
        lanes = slice(j * lane, (j + 1) * lane)
        for b in range(nb):
            glu_ref[j, b * pitch:b * pitch + ts, :] = glu[b * ts:(b + 1) * ts, lanes]
        for t in range(ts):
            ext_ref[j, (CONV_HALO + t) * nb:(CONV_HALO + t + 1) * nb, :] = (
                glu_ref[j, pl.ds(t, nb, stride=pitch), :])

        bias = jnp.broadcast_to(dwb_ref[:, lanes], (nb, lane))

        def taps(acc, t_first, k_first, n_taps, j=j):
            r0 = (t_first + base + k_first) * nb
            rows = [ext_ref[j, pl.ds(pl.multiple_of(r0 + m * nb, nb), nb), :]
                    for m in range(CONV_UNROLL + n_taps - 1)]
            for kk in range(n_taps):
                w = wb_ref[j, pl.ds(pl.multiple_of((k_first + kk) * nb, nb), nb), :]
                acc = [a + rows[u + kk] * w for u, a in enumerate(acc)]
            return acc

        def body(i, carry, bias=bias, taps=taps, j=j):
            t0 = i * CONV_UNROLL
            acc = taps([bias] * CONV_UNROLL, t0, 0, CONV_HEAD)
            acc = lax.fori_loop(
                0, (CONV_WIDTH - CONV_HEAD) // CONV_GROUP,
                lambda g, a: tuple(taps(list(a), t0, CONV_HEAD + g * CONV_GROUP, CONV_GROUP)),
                tuple(acc))
            for u in range(CONV_UNROLL):
                c_ref[j, pl.ds(pl.multiple_of((t0 + u) * nb, nb), nb), :] = acc[u]
            return carry

        lax.fori_loop(0, ts // CONV_UNROLL, body, 0)

    c = jnp.concatenate(
        [jnp.concatenate([c_ref[j, pl.ds(b, ts, stride=nb), :] for b in range(nb)], axis=0)
         for j in range(n_lt)], axis=1)
    c = _silu(_ln(c, lng_ref[...], lnb_ref[...]))
    o_ref[...] = (x + _dot(c.astype(BF16), wout_ref[...])).reshape(nb, ts, d)

    @pl.when(s == pl.num_programs(0) - 1)
    def _():
        for b in range(nb):
            for j in range(n_lt):
                st_ref[b, :, j * lane:(j + 1) * lane] = (
                    glu_ref[j, b * pitch + ts - CONV_BUF:b * pitch + ts, :])

    ext_ref[:, 0:CONV_HALO * nb, :] = ext_ref[:, ts * nb:(ts + CONV_HALO) * nb, :]


def _conv_p(x, consts):
    nb, s, d = x.shape
    dc = consts[-1].shape[0]
    assert nb == V7X_SUBLANES, "the time-major copy puts one sequence on each sublane"
    ts = TM // nb
    assert s % ts == 0 and ts >= CONV_HALO
    return pl.pallas_call(
        functools.partial(_conv_p_kernel, nb=nb, ts=ts),
        grid=(s // ts,),
        in_specs=[pl.BlockSpec((nb, ts, d), lambda j: (0, j, 0))] + [_const_spec(c.shape) for c in consts],
        out_specs=[
            pl.BlockSpec((nb, ts, d), lambda j: (0, j, 0)),
            pl.BlockSpec((nb, CONV_BUF, dc), lambda j: (0, 0, 0)),
        ],
        out_shape=[
            jax.ShapeDtypeStruct((nb, s, d), F32),
            jax.ShapeDtypeStruct((nb, CONV_BUF, dc), F32),
        ],
        scratch_shapes=[
            pltpu.VMEM((dc // V7X_LANES, nb * (ts + V7X_SUBLANES), V7X_LANES), F32),
            pltpu.VMEM((dc // V7X_LANES, (CONV_HALO + ts) * nb, V7X_LANES), F32),
            pltpu.VMEM((dc // V7X_LANES, ts * nb, V7X_LANES), F32),
            pltpu.VMEM((dc // V7X_LANES, CONV_WIDTH * nb, V7X_LANES), F32),
        ],
        compiler_params=_params(1),
        name="conv_p",
    )(x, *consts)


def _gm_p_kernel(x_ref, g_ref, win_ref, lng_ref, lnb_ref, ws_ref, bs_ref, wout_ref,
                 o_ref, v_ref, *, tm):
    s = pl.program_id(1)
    dg = wout_ref.shape[0]
    grp = dg // N_GM_GROUPS
    row = lax.broadcasted_iota(jnp.int32, (CHUNK, CHUNK), 0)
    col = lax.broadcasted_iota(jnp.int32, (CHUNK, CHUNK), 1)
    causal = col <= row
    x = x_ref[...]
    h = _rms(x, g_ref[...]).astype(BF16)
    z = _gelu_tanh(_dot(h, win_ref[...]))
    u = z[:, :dg]
    v = _ln(z[:, dg:], lng_ref[...], lnb_ref[...])
    vb = v.astype(BF16)
    mixed_cols = []
    for gi in range(N_GM_GROUPS):
        wsg = jnp.where(causal, ws_ref[gi], 0.0).astype(BF16)
        bias = bs_ref[:, gi:gi + 1]
        parts = [_dot(wsg, vb[c0:c0 + CHUNK, gi * grp:(gi + 1) * grp]) + bias
                 for c0 in range(0, tm, CHUNK)]
        mixed_cols.append(jnp.concatenate(parts, axis=0))
    mixed = jnp.concatenate(mixed_cols, axis=1)
    y = (u * mixed).astype(BF16)
    o_ref[...] = x + _dot(y, wout_ref[...])

    @pl.when(s == pl.num_programs(1) - 1)
    def _():
        v_ref[...] = v[tm - CHUNK:, :]


def _sc_p_kernel(x_ref, g_ref, win_ref, cw_ref, wout_ref, o_ref, st_ref, ext_ref, *, tm):
    s = pl.program_id(1)
    d = wout_ref.shape[0]

    @pl.when(s == 0)
    def _():
        ext_ref[0:SC_HALO, :] = jnp.zeros((SC_HALO, d), F32)

    x = x_ref[...]
    h = _rms(x, g_ref[...]).astype(BF16)
    z = _dot(h, win_ref[...])
    cx = z[:, d:2 * d] * z[:, 2 * d:]
    ext_ref[SC_HALO:SC_HALO + tm, :] = cx
    conv = cx * cw_ref[SC_WIDTH - 1:SC_WIDTH, :]
    for k in range(SC_WIDTH - 1):
        off = SC_HALO - (SC_WIDTH - 1) + k
        conv = conv + ext_ref[off:off + tm, :] * cw_ref[k:k + 1, :]
    y = (z[:, :d] * conv).astype(BF16)
    o_ref[...] = x + _dot(y, wout_ref[...])

    @pl.when(s == pl.num_programs(1) - 1)
    def _():
        st_ref[...] = ext_ref[SC_HALO + tm - SC_BUF:SC_HALO + tm, :]

    _carry_halo(ext_ref, SC_HALO, tm)


def _prompt_mixer(kern, x, consts, *, state_rows, scratch, name):
    b, s, d = x.shape
    tm = min(TM, s)
    return pl.pallas_call(
        functools.partial(kern, tm=tm),
        grid=(b, s // tm),
        in_specs=[pl.BlockSpec((None, tm, d), lambda i, j: (i, j, 0))]
        + [_const_spec(c.shape) for c in consts],
        out_specs=[
            pl.BlockSpec((None, tm, d), lambda i, j: (i, j, 0)),
            pl.BlockSpec((None, state_rows, d), lambda i, j: (i, 0, 0)),
        ],
        out_shape=[
            jax.ShapeDtypeStruct((b, s, d), F32),
            jax.ShapeDtypeStruct((b, state_rows, d), F32),
        ],
        scratch_shapes=[pltpu.VMEM(shape, F32) for shape in scratch(tm, d)],
        compiler_params=_params(2),
        name=name,
    )(x, *consts)


def _slab(ref, j, d):
    return ref[:, j * d:(j + 1) * d]


def _pool_s_kernel(x_ref, st_ref, g_ref, w_ref, sc_ref, o_ref, nst_ref, d_ref, *, n_new, pos0):
    nb = x_ref.shape[0]
    d = g_ref.shape[1]
    grp = w_ref.shape[1]
    xs = [_slab(x_ref, l, d) for l in range(n_new)]
    hs = [_rms(x, g_ref[...]) for x in xs]

    def ext(j, c0):
        if j < POOL_BUF:
            return st_ref[:, j * d + c0:j * d + c0 + grp]
        return hs[j - POOL_BUF][:, c0:c0 + grp]

    for l in range(n_new):
        for gi, win in enumerate(POOL_WINDOWS):
            c0 = gi * grp
            acc = ext(POOL_BUF + l, c0)
            for k in range(1, win):
                acc = acc + ext(POOL_BUF + l - k, c0)
            cnt = float(min(win, pos0 + l + 1))
            d_ref[l * nb:(l + 1) * nb, c0:c0 + grp] = acc / cnt - hs[l][:, c0:c0 + grp]
    db = d_ref[...].astype(BF16)
    y = jnp.concatenate(
        [_dot(db[:, gi * grp:(gi + 1) * grp], w_ref[gi]) for gi in range(len(POOL_WINDOWS))], axis=1)
    y = y * sc_ref[...]
    for l in range(n_new):
        o_ref[l * nb:(l + 1) * nb, :] = xs[l] + y[l * nb:(l + 1) * nb, :]
    for j in range(POOL_BUF):
        src = j + n_new
        nst_ref[:, j * d:(j + 1) * d] = _slab(st_ref, src, d) if src < POOL_BUF else hs[src - POOL_BUF]


def _conv_s_kernel(x_ref, st_ref, g_ref, win_ref, bin_ref, dw_ref, dwb_ref, lng_ref, lnb_ref,
                   wout_ref, o_ref, nst_ref, g_scr, c_scr, *, n_new):
    nb = st_ref.shape[0]
    dc = wout_ref.shape[0]
    x = x_ref[...]
    h = _rms(x, g_ref[...]).astype(BF16)
    z = _dot(h, win_ref[...]) + bin_ref[...]
    g_scr[...] = z[:, :dc] * _sigmoid(z[:, dc:])

    def ext(j):
        if j < CONV_BUF:
            return _slab(st_ref, j, dc)
        return g_scr[(j - CONV_BUF) * nb:(j - CONV_BUF + 1) * nb, :]

    for l in range(n_new):
        acc = jnp.zeros((nb, dc), F32) + dwb_ref[...]
        for k in range(CONV_WIDTH):
            acc = acc + ext(l + k) * dw_ref[k:k + 1, :]
        c_scr[l * nb:(l + 1) * nb, :] = acc
    c = _silu(_ln(c_scr[...], lng_ref[...], lnb_ref[...]))
    o_ref[...] = x + _dot(c.astype(BF16), wout_ref[...])
    for j in range(CONV_BUF):
        nst_ref[:, j * dc:(j + 1) * dc] = ext(j + n_new)


def _gm_s_kernel(x_ref, g_ref, win_ref, lng_ref, lnb_ref, wrow_ref, brow_ref, wout_ref,
                 o_ref, v_ref, *, n_new, nb):
    dg = wout_ref.shape[0]
    x = x_ref[...]
    h = _rms(x, g_ref[...]).astype(BF16)
    z = _gelu_tanh(_dot(h, win_ref[...]))
    u = z[:, :dg]
    v = _ln(z[:, dg:], lng_ref[...], lnb_ref[...])
    mixed = []
    for t in range(n_new):
        acc = jnp.zeros((nb, dg), F32) + brow_ref[t:t + 1, :]
        for s in range(t + 1):
            acc = acc + wrow_ref[t * n_new + s:t * n_new + s + 1, :] * v[s * nb:(s + 1) * nb, :]
        mixed.append(acc)
    y = (u * jnp.concatenate(mixed, axis=0)).astype(BF16)
    o_ref[...] = x + _dot(y, wout_ref[...])
    for l in range(n_new):
        v_ref[:, l * dg:(l + 1) * dg] = v[l * nb:(l + 1) * nb, :]


def _sc_s_kernel(x_ref, st_ref, g_ref, win_ref, cw_ref, wout_ref, o_ref, nst_ref, *, n_new):
    nb = st_ref.shape[0]
    d = wout_ref.shape[0]
    x = x_ref[...]
    h = _rms(x, g_ref[...]).astype(BF16)
    z = _dot(h, win_ref[...])
    cx = z[:, d:2 * d] * z[:, 2 * d:]

    def ext(j):
        if j < SC_BUF:
            return _slab(st_ref, j, d)
        return cx[(j - SC_BUF) * nb:(j - SC_BUF + 1) * nb, :]

    convs = []
    for l in range(n_new):
        acc = ext(l) * cw_ref[0:1, :]
        for k in range(1, SC_WIDTH):
            acc = acc + ext(l + k) * cw_ref[k:k + 1, :]
        convs.append(acc)
    y = (z[:, :d] * jnp.concatenate(convs, axis=0)).astype(BF16)
    o_ref[...] = x + _dot(y, wout_ref[...])
    for j in range(SC_BUF):
        nst_ref[:, j * d:(j + 1) * d] = ext(j + n_new)


def _single_step(kern, operands, out_shape, scratch, name):
    return pl.pallas_call(
        kern,
        out_shape=out_shape,
        scratch_shapes=scratch,
        compiler_params=pltpu.CompilerParams(vmem_limit_bytes=VMEM_LIMIT),
        name=name,
    )(*operands)


def kernel(x_prompt, x_sample, state_pool, state_conv, state_shortconv, norm_mix, norm_ffn, norm_final, pool_w, pool_scale, conv_w_in, conv_b_in, conv_dw, conv_dw_b, conv_ln_g, conv_ln_b, conv_w_out, gm_w_in, gm_ln_g, gm_ln_b, gm_w_s, gm_b_s, gm_w_out, sc_w_in, sc_conv, sc_w_out, ffn_w_in, ffn_w_out):
    bp, sp, d = x_prompt.shape
    nb, n_new, _ = x_sample.shape
    depth = norm_mix.shape[0]
    assert depth == 4 and pool_w.shape[0] == 1, "one layer of each mixer type"
    assert sp % CHUNK == 0 and n_new < CHUNK and PAST_LEN % CHUNK == 0

    row = lambda a: a.reshape(1, -1)
    bf = lambda a: a.astype(BF16)
    gfin = row(norm_final)
    gffn = norm_ffn.reshape(depth, 1, d)
    grp = d // N_GM_GROUPS
    m = nb * n_new
    tm_shape = jax.ShapeDtypeStruct((m, d), F32)

    def ffn(x, xs, i, w_in, w_out, casts):
        out = _ffn(x.reshape(bp * sp, d), xs, gffn, w_in, w_out, gfin, casts,
                   layer=i, final=(i == depth - 1))
        return out[0].reshape(bp, sp, d), out[1], out[2:]

    pool_consts = (row(norm_mix[0]), bf(pool_w[0]), row(pool_scale[0]))
    x, pool_p = _prompt_mixer(
        functools.partial(_pool_p_kernel, pos0=0), x_prompt, pool_consts, state_rows=POOL_BUF,
        scratch=lambda tm, dd: [(POOL_HALO + tm, dd)], name="pool_p")
    xs, pool_s = _single_step(
        functools.partial(_pool_s_kernel, n_new=n_new, pos0=PAST_LEN),
        (x_sample.reshape(nb, n_new * d), state_pool[0].reshape(nb, POOL_BUF * d)) + pool_consts,
        [tm_shape, jax.ShapeDtypeStruct((nb, POOL_BUF * d), F32)],
        [pltpu.VMEM((m, d), F32)], "pool_s")
    x, xs, (cw_in, cw_out, f_in, f_out) = ffn(
        x, xs, 0, bf(ffn_w_in[0]), bf(ffn_w_out[0]),
        [(conv_w_in, 0), (conv_w_out, 0), (ffn_w_in, 1), (ffn_w_out, 1)])

    conv_consts = (row(norm_mix[1]), cw_in, row(conv_b_in[0]), conv_dw[0], row(conv_dw_b[0]),
                   row(conv_ln_g[0]), row(conv_ln_b[0]), cw_out)
    x, conv_p = _conv_p(x, conv_consts)
    xs, conv_s = _single_step(
        functools.partial(_conv_s_kernel, n_new=n_new),
        (xs, state_conv[0].reshape(nb, CONV_BUF * d)) + conv_consts,
        [tm_shape, jax.ShapeDtypeStruct((nb, CONV_BUF * d), F32)],
        [pltpu.VMEM((m, d), F32), pltpu.VMEM((m, d), F32)], "conv_s")
    x, xs, (gw_in, gw_out, f_in, f_out) = ffn(
        x, xs, 1, f_in, f_out, [(gm_w_in, 0), (gm_w_out, 0), (ffn_w_in, 2), (ffn_w_out, 2)])

    gm_head = (row(norm_mix[2]), gw_in, row(gm_ln_g[0]), row(gm_ln_b[0]))
    x, v_p = _prompt_mixer(
        _gm_p_kernel, x, gm_head + (gm_w_s[0], gm_b_s[0].T, gw_out), state_rows=CHUNK,
        scratch=lambda tm, dd: [], name="gm_p")
    wrow = jnp.repeat(jnp.transpose(gm_w_s[0][:, :n_new, :n_new], (1, 2, 0)).reshape(n_new * n_new, N_GM_GROUPS), grp, axis=1)
    brow = jnp.repeat(gm_b_s[0][:, :n_new].T, grp, axis=1)
    xs, v_s = _single_step(
        functools.partial(_gm_s_kernel, n_new=n_new, nb=nb),
        (xs,) + gm_head + (wrow, brow, gw_out),
        [tm_shape, jax.ShapeDtypeStruct((nb, n_new * d), F32)],
        [], "gm_s")
    x, xs, (sw_in, sw_out, f_in, f_out) = ffn(
        x, xs, 2, f_in, f_out, [(sc_w_in, 0), (sc_w_out, 0), (ffn_w_in, 3), (ffn_w_out, 3)])

    sc_consts = (row(norm_mix[3]), sw_in, sc_conv[0], sw_out)
    x, sc_p = _prompt_mixer(
        _sc_p_kernel, x, sc_consts, state_rows=SC_BUF,
        scratch=lambda tm, dd: [(SC_HALO + tm, dd)], name="sc_p")
    xs, sc_s = _single_step(
        functools.partial(_sc_s_kernel, n_new=n_new),
        (xs, state_shortconv[0].reshape(nb, SC_BUF * d)) + sc_consts,
        [tm_shape, jax.ShapeDtypeStruct((nb, SC_BUF * d), F32)],
        [], "sc_s")
    y_prompt, ys, _ = ffn(x, xs, 3, f_in, f_out, [])
    y_sample = jnp.transpose(ys.reshape(n_new, nb, d), (1, 0, 2))

    return (y_prompt, y_sample,
            pool_p[None], pool_s.reshape(1, nb, POOL_BUF, d),
            conv_p[None], conv_s.reshape(1, nb, CONV_BUF, d),
            v_p[None], v_s.reshape(1, nb, n_new, d),
            sc_p[None], sc_s.reshape(1, nb, SC_BUF, d))
```

```python
import functools

import jax
import jax.numpy as jnp
from jax import lax
from jax.experimental import pallas as pl
from jax.experimental.pallas import tpu as pltpu

F32 = jnp.float32
BF16 = jnp.bfloat16

EPS = 1e-6
POOL_WINDOWS = (2, 4, 8, 16)
POOL_BUF = max(POOL_WINDOWS) - 1
CONV_WIDTH = 31
CONV_BUF = CONV_WIDTH - 1
CHUNK = 128
N_GM_GROUPS = 4
SC_WIDTH = 3
SC_BUF = SC_WIDTH - 1
PAST_LEN = 16384

V7X_SUBLANES = 8
V7X_BF16_SUBLANES = 16
V7X_LANES = 128
V7X_MXU_DIM = 256
V7X_VMEM_BYTES = 64 * 1024 * 1024
VMEM_LIMIT = V7X_VMEM_BYTES - 4 * 1024 * 1024

TM = 1024
POOL_HALO = 32
CONV_HALO = 32
SC_HALO = 8
CONV_UNROLL = 16
CONV_GROUP = 8
CONV_HEAD = CONV_WIDTH % CONV_GROUP


def _params(n_grid):
    return pltpu.CompilerParams(
        dimension_semantics=("arbitrary",) * n_grid, vmem_limit_bytes=VMEM_LIMIT)


def _const_spec(shape):
    nd = len(shape)
    return pl.BlockSpec(shape, lambda *_: (0,) * nd, pipeline_mode=pl.Buffered(1))


def _rms(x, g):
    y = x * lax.rsqrt(jnp.mean(x * x, axis=-1, keepdims=True) + EPS)
    return y * g


def _ln(x, g, b):
    mu = jnp.mean(x, axis=-1, keepdims=True)
    xc = x - mu
    y = xc * lax.rsqrt(jnp.mean(xc * xc, axis=-1, keepdims=True) + EPS)
    return y * g + b


def _sigmoid(x):
    return 0.5 * (1.0 + jnp.tanh(0.5 * x))


def _silu(x):
    return x * _sigmoid(x)


def _gelu_tanh(x):
    c = 0.7978845608028654
    return x * (0.5 * (1.0 + jnp.tanh(c * (x + 0.044715 * (x * x * x)))))


def _dot(a, b):
    return jnp.dot(a, b, preferred_element_type=F32)


def _ffn_chunks(d_ff, n_chunks):
    tiles = d_ff // V7X_MXU_DIM
    assert tiles * V7X_MXU_DIM == d_ff
    edges = [V7X_MXU_DIM * ((tiles * i + n_chunks - 1) // n_chunks) for i in range(n_chunks + 1)]
    return list(zip(edges[:-1], edges[1:]))


def _ffn_kernel(*refs, d_ff, chunks, final, n_cast, n_prompt_steps):
    xp_ref, xs_ref, g_ref, win_ref, wout_ref, gf_ref = refs[:6]
    cast_src = refs[6:6 + n_cast]
    op_ref, os_ref = refs[6 + n_cast:8 + n_cast]
    cast_dst = refs[8 + n_cast:]
    i = pl.program_id(0)

    def rows(x_ref, o_ref):
        x = x_ref[...]
        h = _rms(x, g_ref[...]).astype(BF16)
        acc = x
        for c0, c1 in chunks:
            gate = _dot(h, win_ref[:, c0:c1])
            up = _dot(h, win_ref[:, d_ff + c0:d_ff + c1])
            act = (_silu(gate) * up).astype(BF16)
            acc = acc + _dot(act, wout_ref[c0:c1, :])
        if final:
            acc = _rms(acc, gf_ref[...])
        o_ref[...] = acc

    @pl.when(i < n_prompt_steps)
    def _():
        rows(xp_ref, op_ref)
        for src, dst in zip(cast_src, cast_dst):
            dst[...] = src[...].astype(BF16)

    @pl.when(i == n_prompt_steps)
    def _():
        rows(xs_ref, os_ref)


def _layer_spec(shape, layer):
    nd = len(shape)
    return pl.BlockSpec((None,) + tuple(shape[1:]), lambda *_: (layer,) + (0,) * (nd - 1),
                        pipeline_mode=pl.Buffered(1))


def _cast_specs(casts, n_blocks, block_of):
    cast_in, cast_out, cast_shape = [], [], []
    for arr, idx in casts:
        _, r, c = arr.shape
        assert r % (n_blocks * V7X_BF16_SUBLANES) == 0
        cast_in.append(pl.BlockSpec((None, r // n_blocks, c),
                                    lambda *g, idx=idx: (idx, block_of(*g), 0)))
        cast_out.append(pl.BlockSpec((r // n_blocks, c), lambda *g: (block_of(*g), 0)))
        cast_shape.append(jax.ShapeDtypeStruct((r, c), BF16))
    return cast_in, cast_out, cast_shape


def _ffn(xp, xs, g, w_in, w_out, g_final, casts, *, layer, final):
    mp, d = xp.shape
    ms = xs.shape[0]
    d_ff = w_out.shape[0]
    assert mp % TM == 0
    n_steps = mp // TM
    last = n_steps - 1
    kern = functools.partial(_ffn_kernel, d_ff=d_ff, chunks=_ffn_chunks(d_ff, 2), final=final,
                             n_cast=len(casts), n_prompt_steps=n_steps)
    cast_in, cast_out, cast_shape = _cast_specs(casts, n_steps, lambda i: jnp.minimum(i, last))
    return pl.pallas_call(
        kern,
        grid=(n_steps + 1,),
        in_specs=[
            pl.BlockSpec((TM, d), lambda i: (jnp.minimum(i, last), 0)),
            _const_spec((ms, d)),
            _layer_spec(g.shape, layer),
            _const_spec(w_in.shape),
            _const_spec(w_out.shape),
            _const_spec((1, d)),
        ] + cast_in,
        out_specs=[
            pl.BlockSpec((TM, d), lambda i: (jnp.minimum(i, last), 0)),
            pl.BlockSpec((ms, d), lambda i: (0, 0)),
        ] + cast_out,
        out_shape=[jax.ShapeDtypeStruct((mp, d), F32), jax.ShapeDtypeStruct((ms, d), F32)] + cast_shape,
        compiler_params=_params(1),
        name="ffn_final" if final else "ffn",
    )(xp, xs, g, w_in, w_out, g_final, *[arr for arr, _ in casts])


def _carry_halo(ext_ref, halo, tm):
    ext_ref[0:halo, :] = ext_ref[tm:tm + halo, :]


def _pool_p_kernel(*refs, tm, pos0, n_cast):
    x_ref, g_ref, w_ref, sc_ref = refs[:4]
    cast_src = refs[4:4 + n_cast]
    o_ref, st_ref = refs[4 + n_cast:6 + n_cast]
    cast_dst = refs[6 + n_cast:6 + 2 * n_cast]
    ext_ref, tmp_ref = refs[6 + 2 * n_cast:]
    s = pl.program_id(1)
    grp = w_ref.shape[1]
    halo = POOL_HALO
    end = halo + tm

    @pl.when(s == 0)
    def _():
        ext_ref[0:halo, :] = jnp.zeros((halo, ext_ref.shape[1]), F32)

    x = x_ref[...]
    h = _rms(x, g_ref[...])
    ext_ref[halo:end, :] = h
    pos = pos0 + s * tm + lax.broadcasted_iota(jnp.int32, (tm, 1), 0)
    ys = []
    for gi, win in enumerate(POOL_WINDOWS):
        lanes = slice(gi * grp, (gi + 1) * grp)
        assert win == 1 << (win.bit_length() - 1) and V7X_SUBLANES * (win.bit_length() - 1) <= halo
        read = lambda r0, r1: ext_ref[r0:r1, lanes]
        shift, level = 1, 1
        while 2 * shift < win:
            r0 = V7X_SUBLANES * level
            slot = level % 2
            tmp_ref[slot, r0:end, :] = read(r0, end) + read(r0 - shift, end - shift)
            read = lambda r0, r1, slot=slot: tmp_ref[slot, r0:r1, :]
            shift, level = 2 * shift, level + 1
        acc = read(halo, end) + read(halo - shift, end - shift)
        inv_cnt = 1.0 / jnp.minimum(win, pos + 1).astype(F32)
        d = acc * inv_cnt - h[:, lanes]
        ys.append(_dot(d.astype(BF16), w_ref[gi]))
    y = jnp.concatenate(ys, axis=1)
    o_ref[...] = x + y * sc_ref[...]
    for src, dst in zip(cast_src, cast_dst):
        dst[...] = src[...].astype(BF16)

    @pl.when(s == pl.num_programs(1) - 1)
    def _():
        st_ref[...] = ext_ref[end - POOL_BUF:end, :]

    _carry_halo(ext_ref, halo, tm)


def _conv_p_kernel(x_ref, g_ref, win_ref, bin_ref, dw_ref, dwb_ref, lng_ref, lnb_ref, wout_ref,
                   o_ref, st_ref, glu_ref, ext_ref, c_ref, wb_ref, *, nb, ts):
    s = pl.program_id(0)
    d = x_ref.shape[2]
    n_lt, _, lane = glu_ref.shape
    pitch = glu_ref.shape[1] // nb

    @pl.when(s == 0)
    def _():
        ext_ref[:, 0:CONV_HALO * nb, :] = jnp.zeros((n_lt, CONV_HALO * nb, lane), F32)
        for j in range(n_lt):
            for k in range(CONV_WIDTH):
                wb_ref[j, k * nb:(k + 1) * nb, :] = jnp.broadcast_to(
                    dw_ref[k:k + 1, j * lane:(j + 1) * lane], (nb, lane))

    x = x_ref[...].reshape(nb * ts, d)
    h = _rms(x, g_ref[...]).astype(BF16)
    z = _dot(h, win_ref[...]) + bin_ref[...]
    dc = n_lt * lane
    glu = z[:, :dc] * _sigmoid(z[:, dc:])
    base = CONV_HALO - CONV_BUF
    for j in range(n_lt):
        lanes = slice(j * lane, (j + 1) * lane)
        for b in range(nb):
            glu_ref[j, b * pitch:b * pitch + ts, :] = glu[b * ts:(b + 1) * ts, lanes]
        for t in range(ts):
            ext_ref[j, (CONV_HALO + t) * nb:(CONV_HALO + t + 1) * nb, :] = (
                glu_ref[j, pl.ds(t, nb, stride=pitch), :])

        bias = jnp.broadcast_to(dwb_ref[:, lanes], (nb, lane))

        def taps(acc, t_first, k_first, n_taps, j=j):
            r0 = (t_first + base + k_first) * nb
            rows = [ext_ref[j, pl.ds(pl.multiple_of(r0 + m * nb, nb), nb), :]
                    for m in range(CONV_UNROLL + n_taps - 1)]
            for kk in range(n_taps):
                w = wb_ref[j, pl.ds(pl.multiple_of((k_first + kk) * nb, nb), nb), :]
                acc = [a + rows[u + kk] * w for u, a in enumerate(acc)]
            return acc

        def body(i, carry, bias=bias, taps=taps, j=j):
            t0 = i * CONV_UNROLL
            acc = taps([bias] * CONV_UNROLL, t0, 0, CONV_HEAD)
            acc = lax.fori_loop(
                0, (CONV_WIDTH - CONV_HEAD) // CONV_GROUP,
                lambda g, a: tuple(taps(list(a), t0, CONV_HEAD + g * CONV_GROUP, CONV_GROUP)),
                tuple(acc))
            for u in range(CONV_UNROLL):
                c_ref[j, pl.ds(pl.multiple_of((t0 + u) * nb, nb), nb), :] = acc[u]
            return carry

        lax.fori_loop(0, ts // CONV_UNROLL, body, 0)

    c = jnp.concatenate(
        [jnp.concatenate([c_ref[j, pl.ds(b, ts, stride=nb), :] for b in range(nb)], axis=0)
         for j in range(n_lt)], axis=1)
    c = _silu(_ln(c, lng_ref[...], lnb_ref[...]))
    o_ref[...] = (x + _dot(c.astype(BF16), wout_ref[...])).reshape(nb, ts, d)

    @pl.when(s == pl.num_programs(0) - 1)
    def _():
        for b in range(nb):
            for j in range(n_lt):
                st_ref[b, :, j * lane:(j + 1) * lane] = (
                    glu_ref[j, b * pitch + ts - CONV_BUF:b * pitch + ts, :])

    ext_ref[:, 0:CONV_HALO * nb, :] = ext_ref[:, ts * nb:(ts + CONV_HALO) * nb, :]


def _conv_p(x, consts):
    nb, s, d = x.shape
    dc = consts[-1].shape[0]
    assert nb == V7X_SUBLANES, "the time-major copy puts one sequence on each sublane"
    ts = TM // nb
    assert s % ts == 0 and ts >= CONV_HALO
    return pl.pallas_call(
        functools.partial(_conv_p_kernel, nb=nb, ts=ts),
        grid=(s // ts,),
        in_specs=[pl.BlockSpec((nb, ts, d), lambda j: (0, j, 0))] + [_const_spec(c.shape) for c in consts],
        out_specs=[
            pl.BlockSpec((nb, ts, d), lambda j: (0, j, 0)),
            pl.BlockSpec((nb, CONV_BUF, dc), lambda j: (0, 0, 0)),
        ],
        out_shape=[
            jax.ShapeDtypeStruct((nb, s, d), F32),
            jax.ShapeDtypeStruct((nb, CONV_BUF, dc), F32),
        ],
        scratch_shapes=[
            pltpu.VMEM((dc // V7X_LANES, nb * (ts + V7X_SUBLANES), V7X_LANES), F32),
            pltpu.VMEM((dc // V7X_LANES, (CONV_HALO + ts) * nb, V7X_LANES), F32),
            pltpu.VMEM((dc // V7X_LANES, ts * nb, V7X_LANES), F32),
            pltpu.VMEM((dc // V7X_LANES, CONV_WIDTH * nb, V7X_LANES), F32),
        ],
        compiler_params=_params(1),
        name="conv_p",
    )(x, *consts)


def _gm_p_kernel(x_ref, g_ref, win_ref, lng_ref, lnb_ref, ws_ref, bs_ref, wout_ref,
                 o_ref, v_ref, *, tm):
    s = pl.program_id(1)
    dg = wout_ref.shape[0]
    grp = dg // N_GM_GROUPS
    row = lax.broadcasted_iota(jnp.int32, (CHUNK, CHUNK), 0)
    col = lax.broadcasted_iota(jnp.int32, (CHUNK, CHUNK), 1)
    causal = col <= row
    x = x_ref[...]
    h = _rms(x, g_ref[...]).astype(BF16)
    z = _gelu_tanh(_dot(h, win_ref[...]))
    u = z[:, :dg]
    v = _ln(z[:, dg:], lng_ref[...], lnb_ref[...])
    vb = v.astype(BF16)
    mixed_cols = []
    for gi in range(N_GM_GROUPS):
        wsg = jnp.where(causal, ws_ref[gi], 0.0).astype(BF16)
        bias = bs_ref[:, gi:gi + 1]
        parts = [_dot(wsg, vb[c0:c0 + CHUNK, gi * grp:(gi + 1) * grp]) + bias
                 for c0 in range(0, tm, CHUNK)]
        mixed_cols.append(jnp.concatenate(parts, axis=0))
    mixed = jnp.concatenate(mixed_cols, axis=1)
    y = (u * mixed).astype(BF16)
    o_ref[...] = x + _dot(y, wout_ref[...])

    @pl.when(s == pl.num_programs(1) - 1)
    def _():
        v_ref[...] = v[tm - CHUNK:, :]


def _sc_p_kernel(x_ref, g_ref, win_ref, cw_ref, wout_ref, o_ref, st_ref, ext_ref, *, tm):
    s = pl.program_id(1)
    d = wout_ref.shape[0]

    @pl.when(s == 0)
    def _():
        ext_ref[0:SC_HALO, :] = jnp.zeros((SC_HALO, d), F32)

    x = x_ref[...]
    h = _rms(x, g_ref[...]).astype(BF16)
    z = _dot(h, win_ref[...])
    cx = z[:, d:2 * d] * z[:, 2 * d:]
    ext_ref[SC_HALO:SC_HALO + tm, :] = cx
    conv = cx * cw_ref[SC_WIDTH - 1:SC_WIDTH, :]
    for k in range(SC_WIDTH - 1):
        off = SC_HALO - (SC_WIDTH - 1) + k
        conv = conv + ext_ref[off:off + tm, :] * cw_ref[k:k + 1, :]
    y = (z[:, :d] * conv).astype(BF16)
    o_ref[...] = x + _dot(y, wout_ref[...])

    @pl.when(s == pl.num_programs(1) - 1)
    def _():
        st_ref[...] = ext_ref[SC_HALO + tm - SC_BUF:SC_HALO + tm, :]

    _carry_halo(ext_ref, SC_HALO, tm)


def _prompt_mixer(kern, x, consts, *, state_rows, scratch, name, casts=()):
    b, s, d = x.shape
    tm = min(TM, s)
    n_j = s // tm
    cast_in, cast_out, cast_shape = _cast_specs(casts, b * n_j, lambda i, j: i * n_j + j)
    if casts:
        kern = functools.partial(kern, n_cast=len(casts))
    return pl.pallas_call(
        functools.partial(kern, tm=tm),
        grid=(b, n_j),
        in_specs=[pl.BlockSpec((None, tm, d), lambda i, j: (i, j, 0))]
        + [_const_spec(c.shape) for c in consts] + cast_in,
        out_specs=[
            pl.BlockSpec((None, tm, d), lambda i, j: (i, j, 0)),
            pl.BlockSpec((None, state_rows, d), lambda i, j: (i, 0, 0)),
        ] + cast_out,
        out_shape=[
            jax.ShapeDtypeStruct((b, s, d), F32),
            jax.ShapeDtypeStruct((b, state_rows, d), F32),
        ] + cast_shape,
        scratch_shapes=[pltpu.VMEM(shape, F32) for shape in scratch(tm, d)],
        compiler_params=_params(2),
        name=name,
    )(x, *consts, *[arr for arr, _ in casts])


def _slab(ref, j, d):
    return ref[:, j * d:(j + 1) * d]


def _pool_s_kernel(x_ref, st_ref, g_ref, w_ref, sc_ref, o_ref, nst_ref, d_ref, *, n_new, pos0):
    nb = x_ref.shape[0]
    d = g_ref.shape[1]
    grp = w_ref.shape[1]
    xs = [_slab(x_ref, l, d) for l in range(n_new)]
    hs = [_rms(x, g_ref[...]) for x in xs]

    def ext(j, c0):
        if j < POOL_BUF:
            return st_ref[:, j * d + c0:j * d + c0 + grp]
        return hs[j - POOL_BUF][:, c0:c0 + grp]

    for l in range(n_new):
        for gi, win in enumerate(POOL_WINDOWS):
            c0 = gi * grp
            acc = ext(POOL_BUF + l, c0)
            for k in range(1, win):
                acc = acc + ext(POOL_BUF + l - k, c0)
            cnt = float(min(win, pos0 + l + 1))
            d_ref[l * nb:(l + 1) * nb, c0:c0 + grp] = acc / cnt - hs[l][:, c0:c0 + grp]
    db = d_ref[...].astype(BF16)
    y = jnp.concatenate(
        [_dot(db[:, gi * grp:(gi + 1) * grp], w_ref[gi]) for gi in range(len(POOL_WINDOWS))], axis=1)
    y = y * sc_ref[...]
    for l in range(n_new):
        o_ref[l * nb:(l + 1) * nb, :] = xs[l] + y[l * nb:(l + 1) * nb, :]
    for j in range(POOL_BUF):
        src = j + n_new
        nst_ref[:, j * d:(j + 1) * d] = _slab(st_ref, src, d) if src < POOL_BUF else hs[src - POOL_BUF]


def _conv_s_kernel(x_ref, st_ref, g_ref, win_ref, bin_ref, dw_ref, dwb_ref, lng_ref, lnb_ref,
                   wout_ref, o_ref, nst_ref, g_scr, c_scr, *, n_new):
    nb = st_ref.shape[0]
    dc = wout_ref.shape[0]
    x = x_ref[...]
    h = _rms(x, g_ref[...]).astype(BF16)
    z = _dot(h, win_ref[...]) + bin_ref[...]
    g_scr[...] = z[:, :dc] * _sigmoid(z[:, dc:])

    def ext(j):
        if j < CONV_BUF:
            return _slab(st_ref, j, dc)
        return g_scr[(j - CONV_BUF) * nb:(j - CONV_BUF + 1) * nb, :]

    for l in range(n_new):
        acc = jnp.zeros((nb, dc), F32) + dwb_ref[...]
        for k in range(CONV_WIDTH):
            acc = acc + ext(l + k) * dw_ref[k:k + 1, :]
        c_scr[l * nb:(l + 1) * nb, :] = acc
    c = _silu(_ln(c_scr[...], lng_ref[...], lnb_ref[...]))
    o_ref[...] = x + _dot(c.astype(BF16), wout_ref[...])
    for j in range(CONV_BUF):
        nst_ref[:, j * dc:(j + 1) * dc] = ext(j + n_new)


def _gm_s_kernel(x_ref, g_ref, win_ref, lng_ref, lnb_ref, wrow_ref, brow_ref, wout_ref,
                 o_ref, v_ref, *, n_new, nb):
    dg = wout_ref.shape[0]
    x = x_ref[...]
    h = _rms(x, g_ref[...]).astype(BF16)
    z = _gelu_tanh(_dot(h, win_ref[...]))
    u = z[:, :dg]
    v = _ln(z[:, dg:], lng_ref[...], lnb_ref[...])
    mixed = []
    for t in range(n_new):
        acc = jnp.zeros((nb, dg), F32) + brow_ref[t:t + 1, :]
        for s in range(t + 1):
            acc = acc + wrow_ref[t * n_new + s:t * n_new + s + 1, :] * v[s * nb:(s + 1) * nb, :]
        mixed.append(acc)
    y = (u * jnp.concatenate(mixed, axis=0)).astype(BF16)
    o_ref[...] = x + _dot(y, wout_ref[...])
    for l in range(n_new):
        v_ref[:, l * dg:(l + 1) * dg] = v[l * nb:(l + 1) * nb, :]


def _sc_s_kernel(x_ref, st_ref, g_ref, win_ref, cw_ref, wout_ref, o_ref, nst_ref, *, n_new):
    nb = st_ref.shape[0]
    d = wout_ref.shape[0]
    x = x_ref[...]
    h = _rms(x, g_ref[...]).astype(BF16)
    z = _dot(h, win_ref[...])
    cx = z[:, d:2 * d] * z[:, 2 * d:]

    def ext(j):
        if j < SC_BUF:
            return _slab(st_ref, j, d)
        return cx[(j - SC_BUF) * nb:(j - SC_BUF + 1) * nb, :]

    convs = []
    for l in range(n_new):
        acc = ext(l) * cw_ref[0:1, :]
        for k in range(1, SC_WIDTH):
            acc = acc + ext(l + k) * cw_ref[k:k + 1, :]
        convs.append(acc)
    y = (z[:, :d] * jnp.concatenate(convs, axis=0)).astype(BF16)
    o_ref[...] = x + _dot(y, wout_ref[...])
    for j in range(SC_BUF):
        nst_ref[:, j * d:(j + 1) * d] = ext(j + n_new)


def _single_step(kern, operands, out_shape, scratch, name):
    return pl.pallas_call(
        kern,
        out_shape=out_shape,
        scratch_shapes=scratch,
        compiler_params=pltpu.CompilerParams(vmem_limit_bytes=VMEM_LIMIT),
        name=name,
    )(*operands)


def kernel(x_prompt, x_sample, state_pool, state_conv, state_shortconv, norm_mix, norm_ffn, norm_final, pool_w, pool_scale, conv_w_in, conv_b_in, conv_dw, conv_dw_b, conv_ln_g, conv_ln_b, conv_w_out, gm_w_in, gm_ln_g, gm_ln_b, gm_w_s, gm_b_s, gm_w_out, sc_w_in, sc_conv, sc_w_out, ffn_w_in, ffn_w_out):
    bp, sp, d = x_prompt.shape
    nb, n_new, _ = x_sample.shape
    depth = norm_mix.shape[0]
    assert depth == 4 and pool_w.shape[0] == 1, "one layer of each mixer type"
    assert sp % CHUNK == 0 and n_new < CHUNK and PAST_LEN % CHUNK == 0

    row = lambda a: a.reshape(1, -1)
    bf = lambda a: a.astype(BF16)
    gfin = row(norm_final)
    gffn = norm_ffn.reshape(depth, 1, d)
    grp = d // N_GM_GROUPS
    m = nb * n_new
    tm_shape = jax.ShapeDtypeStruct((m, d), F32)

    def ffn(x, xs, i, w_in, w_out, casts):
        out = _ffn(x.reshape(bp * sp, d), xs, gffn, w_in, w_out, gfin, casts,
                   layer=i, final=(i == depth - 1))
        return out[0].reshape(bp, sp, d), out[1], out[2:]

    pool_consts = (row(norm_mix[0]), bf(pool_w[0]), row(pool_scale[0]))
    x, pool_p, f_in, f_out = _prompt_mixer(
        functools.partial(_pool_p_kernel, pos0=0), x_prompt, pool_consts, state_rows=POOL_BUF,
        scratch=lambda tm, dd: [(POOL_HALO + tm, dd), (2, POOL_HALO + tm, dd // len(POOL_WINDOWS))],
        name="pool_p", casts=[(ffn_w_in, 0), (ffn_w_out, 0)])
    xs, pool_s = _single_step(
        functools.partial(_pool_s_kernel, n_new=n_new, pos0=PAST_LEN),
        (x_sample.reshape(nb, n_new * d), state_pool[0].reshape(nb, POOL_BUF * d)) + pool_consts,
        [tm_shape, jax.ShapeDtypeStruct((nb, POOL_BUF * d), F32)],
        [pltpu.VMEM((m, d), F32)], "pool_s")
    x, xs, (cw_in, cw_out, f_in, f_out) = ffn(
        x, xs, 0, f_in, f_out, [(conv_w_in, 0), (conv_w_out, 0), (ffn_w_in, 1), (ffn_w_out, 1)])

    conv_consts = (row(norm_mix[1]), cw_in, row(conv_b_in[0]), conv_dw[0], row(conv_dw_b[0]),
                   row(conv_ln_g[0]), row(conv_ln_b[0]), cw_out)
    x, conv_p = _conv_p(x, conv_consts)
    xs, conv_s = _single_step(
        functools.partial(_conv_s_kernel, n_new=n_new),
        (xs, state_conv[0].reshape(nb, CONV_BUF * d)) + conv_consts,
        [tm_shape, jax.ShapeDtypeStruct((nb, CONV_BUF * d), F32)],
        [pltpu.VMEM((m, d), F32), pltpu.VMEM((m, d), F32)], "conv_s")
    x, xs, (gw_in, gw_out, f_in, f_out) = ffn(
        x, xs, 1, f_in, f_out, [(gm_w_in, 0), (gm_w_out, 0), (ffn_w_in, 2), (ffn_w_out, 2)])

    gm_head = (row(norm_mix[2]), gw_in, row(gm_ln_g[0]), row(gm_ln_b[0]))
    x, v_p = _prompt_mixer(
        _gm_p_kernel, x, gm_head + (gm_w_s[0], gm_b_s[0].T, gw_out), state_rows=CHUNK,
        scratch=lambda tm, dd: [], name="gm_p")
    wrow = jnp.repeat(jnp.transpose(gm_w_s[0][:, :n_new, :n_new], (1, 2, 0)).reshape(n_new * n_new, N_GM_GROUPS), grp, axis=1)
    brow = jnp.repeat(gm_b_s[0][:, :n_new].T, grp, axis=1)
    xs, v_s = _single_step(
        functools.partial(_gm_s_kernel, n_new=n_new, nb=nb),
        (xs,) + gm_head + (wrow, brow, gw_out),
        [tm_shape, jax.ShapeDtypeStruct((nb, n_new * d), F32)],
        [], "gm_s")
    x, xs, (sw_in, sw_out, f_in, f_out) = ffn(
        x, xs, 2, f_in, f_out, [(sc_w_in, 0), (sc_w_out, 0), (ffn_w_in, 3), (ffn_w_out, 3)])

    sc_consts = (row(norm_mix[3]), sw_in, sc_conv[0], sw_out)
    x, sc_p = _prompt_mixer(
        _sc_p_kernel, x, sc_consts, state_rows=SC_BUF,
        scratch=lambda tm, dd: [(SC_HALO + tm, dd)], name="sc_p")
    xs, sc_s = _single_step(
        functools.partial(_sc_s_kernel, n_new=n_new),
        (xs, state_shortconv[0].reshape(nb, SC_BUF * d)) + sc_consts,
        [tm_shape, jax.ShapeDtypeStruct((nb, SC_BUF * d), F32)],
        [], "sc_s")
    y_prompt, ys, _ = ffn(x, xs, 3, f_in, f_out, [])
    y_sample = jnp.transpose(ys.reshape(n_new, nb, d), (1, 0, 2))

    return (y_prompt, y_sample,
            pool_p[None], pool_s.reshape(1, nb, POOL_BUF, d),
            conv_p[None], conv_s.reshape(1, nb, CONV_BUF, d),
            v_p[None], v_s.reshape(1, nb, n_new, d),
            sc_p[None], sc_s.reshape(1, nb, SC_BUF, d))
```

```python
import functools

import jax
import jax.numpy as jnp
from jax import lax
from jax.experimental import pallas as pl
from jax.experimental.pallas import tpu as pltpu

F32 = jnp.float32
BF16 = jnp.bfloat16

EPS = 1e-6
POOL_WINDOWS = (2, 4, 8, 16)
POOL_BUF = max(POOL_WINDOWS) - 1
CONV_WIDTH = 31
CONV_BUF = CONV_WIDTH - 1
CHUNK = 128
N_GM_GROUPS = 4
SC_WIDTH = 3
SC_BUF = SC_WIDTH - 1
PAST_LEN = 16384

V7X_SUBLANES = 8
V7X_BF16_SUBLANES = 16
V7X_LANES = 128
V7X_MXU_DIM = 256
V7X_VMEM_BYTES = 64 * 1024 * 1024
VMEM_LIMIT = V7X_VMEM_BYTES - 4 * 1024 * 1024

TM = 1024
POOL_HALO = 32
CONV_HALO = 32
SC_HALO = 8
CONV_UNROLL = 16
CONV_GROUP = 8


def _params(n_grid):
    return pltpu.CompilerParams(
        dimension_semantics=("arbitrary",) * n_grid, vmem_limit_bytes=VMEM_LIMIT)


def _const_spec(shape):
    nd = len(shape)
    return pl.BlockSpec(shape, lambda *_: (0,) * nd, pipeline_mode=pl.Buffered(1))


def _rms(x, g):
    y = x * lax.rsqrt(jnp.mean(x * x, axis=-1, keepdims=True) + EPS)
    return y * g


def _ln(x, g, b):
    mu = jnp.mean(x, axis=-1, keepdims=True)
    xc = x - mu
    y = xc * lax.rsqrt(jnp.mean(xc * xc, axis=-1, keepdims=True) + EPS)
    return y * g + b


def _sigmoid(x):
    return 0.5 * (1.0 + jnp.tanh(0.5 * x))


def _silu(x):
    return x * _sigmoid(x)


def _gelu_tanh(x):
    c = 0.7978845608028654
    return x * (0.5 * (1.0 + jnp.tanh(c * (x + 0.044715 * (x * x * x)))))


def _dot(a, b):
    return jnp.dot(a, b, preferred_element_type=F32)


def _ffn_chunks(d_ff, n_chunks):
    tiles = d_ff // V7X_MXU_DIM
    assert tiles * V7X_MXU_DIM == d_ff
    edges = [V7X_MXU_DIM * ((tiles * i + n_chunks - 1) // n_chunks) for i in range(n_chunks + 1)]
    return list(zip(edges[:-1], edges[1:]))


def _ffn_kernel(*refs, d_ff, chunks, final, n_cast, n_prompt_steps):
    xp_ref, xs_ref, g_ref, win_ref, wout_ref, gf_ref = refs[:6]
    cast_src = refs[6:6 + n_cast]
    op_ref, os_ref = refs[6 + n_cast:8 + n_cast]
    cast_dst = refs[8 + n_cast:]
    i = pl.program_id(0)

    def rows(x_ref, o_ref):
        x = x_ref[...]
        h = _rms(x, g_ref[...]).astype(BF16)
        acc = x
        for c0, c1 in chunks:
            gate = _dot(h, win_ref[:, c0:c1])
            up = _dot(h, win_ref[:, d_ff + c0:d_ff + c1])
            act = (_silu(gate) * up).astype(BF16)
            acc = acc + _dot(act, wout_ref[c0:c1, :])
        if final:
            acc = _rms(acc, gf_ref[...])
        o_ref[...] = acc

    @pl.when(i < n_prompt_steps)
    def _():
        rows(xp_ref, op_ref)
        for src, dst in zip(cast_src, cast_dst):
            dst[...] = src[...].astype(BF16)

    @pl.when(i == n_prompt_steps)
    def _():
        rows(xs_ref, os_ref)


def _layer_spec(shape, layer):
    nd = len(shape)
    return pl.BlockSpec((None,) + tuple(shape[1:]), lambda *_: (layer,) + (0,) * (nd - 1),
                        pipeline_mode=pl.Buffered(1))


def _cast_specs(casts, n_blocks, block_of):
    cast_in, cast_out, cast_shape = [], [], []
    for arr, idx in casts:
        _, r, c = arr.shape
        assert r % (n_blocks * V7X_BF16_SUBLANES) == 0
        cast_in.append(pl.BlockSpec((None, r // n_blocks, c),
                                    lambda *g, idx=idx: (idx, block_of(*g), 0)))
        cast_out.append(pl.BlockSpec((r // n_blocks, c), lambda *g: (block_of(*g), 0)))
        cast_shape.append(jax.ShapeDtypeStruct((r, c), BF16))
    return cast_in, cast_out, cast_shape


def _ffn(xp, xs, g, w_in, w_out, g_final, casts, *, layer, final):
    mp, d = xp.shape
    ms = xs.shape[0]
    d_ff = w_out.shape[0]
    assert mp % TM == 0
    n_steps = mp // TM
    last = n_steps - 1
    kern = functools.partial(_ffn_kernel, d_ff=d_ff, chunks=_ffn_chunks(d_ff, 2), final=final,
                             n_cast=len(casts), n_prompt_steps=n_steps)
    cast_in, cast_out, cast_shape = _cast_specs(casts, n_steps, lambda i: jnp.minimum(i, last))
    return pl.pallas_call(
        kern,
        grid=(n_steps + 1,),
        in_specs=[
            pl.BlockSpec((TM, d), lambda i: (jnp.minimum(i, last), 0)),
            _const_spec((ms, d)),
            _layer_spec(g.shape, layer),
            _const_spec(w_in.shape),
            _const_spec(w_out.shape),
            _const_spec((1, d)),
        ] + cast_in,
        out_specs=[
            pl.BlockSpec((TM, d), lambda i: (jnp.minimum(i, last), 0)),
            pl.BlockSpec((ms, d), lambda i: (0, 0)),
        ] + cast_out,
        out_shape=[jax.ShapeDtypeStruct((mp, d), F32), jax.ShapeDtypeStruct((ms, d), F32)] + cast_shape,
        compiler_params=_params(1),
        name="ffn_final" if final else "ffn",
    )(xp, xs, g, w_in, w_out, g_final, *[arr for arr, _ in casts])


def _carry_halo(ext_ref, halo, tm):
    ext_ref[0:halo, :] = ext_ref[tm:tm + halo, :]


def _pool_p_kernel(*refs, tm, pos0, n_cast):
    x_ref, g_ref, w_ref, sc_ref = refs[:4]
    cast_src = refs[4:4 + n_cast]
    o_ref, st_ref = refs[4 + n_cast:6 + n_cast]
    cast_dst = refs[6 + n_cast:6 + 2 * n_cast]
    ext_ref, tmp_ref = refs[6 + 2 * n_cast:]
    s = pl.program_id(1)
    grp = w_ref.shape[1]
    halo = POOL_HALO
    end = halo + tm

    @pl.when(s == 0)
    def _():
        ext_ref[0:halo, :] = jnp.zeros((halo, ext_ref.shape[1]), F32)

    x = x_ref[...]
    h = _rms(x, g_ref[...])
    ext_ref[halo:end, :] = h
    pos = pos0 + s * tm + lax.broadcasted_iota(jnp.int32, (tm, 1), 0)
    ys = []
    for gi, win in enumerate(POOL_WINDOWS):
        lanes = slice(gi * grp, (gi + 1) * grp)
        assert win == 1 << (win.bit_length() - 1) and V7X_SUBLANES * (win.bit_length() - 1) <= halo
        read = lambda r0, r1: ext_ref[r0:r1, lanes]
        shift, level = 1, 1
        while 2 * shift < win:
            r0 = V7X_SUBLANES * level
            slot = level % 2
            tmp_ref[slot, r0:end, :] = read(r0, end) + read(r0 - shift, end - shift)
            read = lambda r0, r1, slot=slot: tmp_ref[slot, r0:r1, :]
            shift, level = 2 * shift, level + 1
        acc = read(halo, end) + read(halo - shift, end - shift)
        inv_cnt = 1.0 / jnp.minimum(win, pos + 1).astype(F32)
        d = acc * inv_cnt - h[:, lanes]
        ys.append(_dot(d.astype(BF16), w_ref[gi]))
    y = jnp.concatenate(ys, axis=1)
    o_ref[...] = x + y * sc_ref[...]
    for src, dst in zip(cast_src, cast_dst):
        dst[...] = src[...].astype(BF16)

    @pl.when(s == pl.num_programs(1) - 1)
    def _():
        st_ref[...] = ext_ref[end - POOL_BUF:end, :]

    _carry_halo(ext_ref, halo, tm)


def _conv_p_kernel(x_ref, g_ref, win_ref, bin_ref, dw_ref, dwb_ref, lng_ref, lnb_ref, wout_ref,
                   o_ref, st_ref, glu_ref, ext_ref, c_ref, wb_ref, *, nb, ts):
    s = pl.program_id(0)
    d = x_ref.shape[2]
    n_lt, _, lane = glu_ref.shape
    pitch = glu_ref.shape[1] // nb

    @pl.when(s == 0)
    def _():
        ext_ref[:, 0:CONV_HALO * nb, :] = jnp.zeros((n_lt, CONV_HALO * nb, lane), F32)
        for j in range(n_lt):
            for k in range(CONV_WIDTH):
                wb_ref[j, k * nb:(k + 1) * nb, :] = jnp.broadcast_to(
                    dw_ref[k:k + 1, j * lane:(j + 1) * lane], (nb, lane))

    x = x_ref[...].reshape(nb * ts, d)
    h = _rms(x, g_ref[...]).astype(BF16)
    z = _dot(h, win_ref[...]) + bin_ref[...]
    dc = n_lt * lane
    glu = z[:, :dc] * _sigmoid(z[:, dc:])
    base = CONV_HALO - CONV_BUF
    for j in range(n_lt):
        lanes = slice(j * lane, (j + 1) * lane)
        for b in range(nb):
            glu_ref[j, b * pitch:b * pitch + ts, :] = glu[b * ts:(b + 1) * ts, lanes]
        for t in range(ts):
            ext_ref[j, (CONV_HALO + t) * nb:(CONV_HALO + t + 1) * nb, :] = (
                glu_ref[j, pl.ds(t, nb, stride=pitch), :])

        bias = jnp.broadcast_to(dwb_ref[:, lanes], (nb, lane))
        for t0 in range(0, ts, CONV_UNROLL):
            acc = [bias] * CONV_UNROLL
            for k0 in range(0, CONV_WIDTH, CONV_GROUP):
                n_taps = min(CONV_GROUP, CONV_WIDTH - k0)
                r0 = t0 + base + k0
                rows = [ext_ref[j, (r0 + m) * nb:(r0 + m + 1) * nb, :]
                        for m in range(CONV_UNROLL + n_taps - 1)]
                for kk in range(n_taps):
                    w = wb_ref[j, (k0 + kk) * nb:(k0 + kk + 1) * nb, :]
                    acc = [a + rows[u + kk] * w for u, a in enumerate(acc)]
            for u in range(CONV_UNROLL):
                c_ref[j, (t0 + u) * nb:(t0 + u + 1) * nb, :] = acc[u]

    c = jnp.concatenate(
        [jnp.concatenate([c_ref[j, pl.ds(b, ts, stride=nb), :] for b in range(nb)], axis=0)
         for j in range(n_lt)], axis=1)
    c = _silu(_ln(c, lng_ref[...], lnb_ref[...]))
    o_ref[...] = (x + _dot(c.astype(BF16), wout_ref[...])).reshape(nb, ts, d)

    @pl.when(s == pl.num_programs(0) - 1)
    def _():
        for b in range(nb):
            for j in range(n_lt):
                st_ref[b, :, j * lane:(j + 1) * lane] = (
                    glu_ref[j, b * pitch + ts - CONV_BUF:b * pitch + ts, :])

    ext_ref[:, 0:CONV_HALO * nb, :] = ext_ref[:, ts * nb:(ts + CONV_HALO) * nb, :]


def _conv_p(x, consts):
    nb, s, d = x.shape
    dc = consts[-1].shape[0]
    assert nb == V7X_SUBLANES, "the time-major copy puts one sequence on each sublane"
    ts = TM // nb
    assert s % ts == 0 and ts >= CONV_HALO
    return pl.pallas_call(
        functools.partial(_conv_p_kernel, nb=nb, ts=ts),
        grid=(s // ts,),
        in_specs=[pl.BlockSpec((nb, ts, d), lambda j: (0, j, 0))] + [_const_spec(c.shape) for c in consts],
        out_specs=[
            pl.BlockSpec((nb, ts, d), lambda j: (0, j, 0)),
            pl.BlockSpec((nb, CONV_BUF, dc), lambda j: (0, 0, 0)),
        ],
        out_shape=[
            jax.ShapeDtypeStruct((nb, s, d), F32),
            jax.ShapeDtypeStruct((nb, CONV_BUF, dc), F32),
        ],
        scratch_shapes=[
            pltpu.VMEM((dc // V7X_LANES, nb * (ts + V7X_SUBLANES), V7X_LANES), F32),
            pltpu.VMEM((dc // V7X_LANES, (CONV_HALO + ts) * nb, V7X_LANES), F32),
            pltpu.VMEM((dc // V7X_LANES, ts * nb, V7X_LANES), F32),
            pltpu.VMEM((dc // V7X_LANES, CONV_WIDTH * nb, V7X_LANES), F32),
        ],
        compiler_params=_params(1),
        name="conv_p",
    )(x, *consts)


def _gm_p_kernel(x_ref, g_ref, win_ref, lng_ref, lnb_ref, ws_ref, bs_ref, wout_ref,
                 o_ref, v_ref, *, tm):
    s = pl.program_id(1)
    dg = wout_ref.shape[0]
    grp = dg // N_GM_GROUPS
    row = lax.broadcasted_iota(jnp.int32, (CHUNK, CHUNK), 0)
    col = lax.broadcasted_iota(jnp.int32, (CHUNK, CHUNK), 1)
    causal = col <= row
    x = x_ref[...]
    h = _rms(x, g_ref[...]).astype(BF16)
    z = _gelu_tanh(_dot(h, win_ref[...]))
    u = z[:, :dg]
    v = _ln(z[:, dg:], lng_ref[...], lnb_ref[...])
    vb = v.astype(BF16)
    mixed_cols = []
    for gi in range(N_GM_GROUPS):
        wsg = jnp.where(causal, ws_ref[gi], 0.0).astype(BF16)
        bias = bs_ref[:, gi:gi + 1]
        parts = [_dot(wsg, vb[c0:c0 + CHUNK, gi * grp:(gi + 1) * grp]) + bias
                 for c0 in range(0, tm, CHUNK)]
        mixed_cols.append(jnp.concatenate(parts, axis=0))
    mixed = jnp.concatenate(mixed_cols, axis=1)
    y = (u * mixed).astype(BF16)
    o_ref[...] = x + _dot(y, wout_ref[...])

    @pl.when(s == pl.num_programs(1) - 1)
    def _():
        v_ref[...] = v[tm - CHUNK:, :]


def _sc_p_kernel(x_ref, g_ref, win_ref, cw_ref, wout_ref, o_ref, st_ref, ext_ref, *, tm):
    s = pl.program_id(1)
    d = wout_ref.shape[0]

    @pl.when(s == 0)
    def _():
        ext_ref[0:SC_HALO, :] = jnp.zeros((SC_HALO, d), F32)

    x = x_ref[...]
    h = _rms(x, g_ref[...]).astype(BF16)
    z = _dot(h, win_ref[...])
    cx = z[:, d:2 * d] * z[:, 2 * d:]
    ext_ref[SC_HALO:SC_HALO + tm, :] = cx
    conv = cx * cw_ref[SC_WIDTH - 1:SC_WIDTH, :]
    for k in range(SC_WIDTH - 1):
        off = SC_HALO - (SC_WIDTH - 1) + k
        conv = conv + ext_ref[off:off + tm, :] * cw_ref[k:k + 1, :]
    y = (z[:, :d] * conv).astype(BF16)
    o_ref[...] = x + _dot(y, wout_ref[...])

    @pl.when(s == pl.num_programs(1) - 1)
    def _():
        st_ref[...] = ext_ref[SC_HALO + tm - SC_BUF:SC_HALO + tm, :]

    _carry_halo(ext_ref, SC_HALO, tm)


def _prompt_mixer(kern, x, consts, *, state_rows, scratch, name, casts=()):
    b, s, d = x.shape
    tm = min(TM, s)
    n_j = s // tm
    cast_in, cast_out, cast_shape = _cast_specs(casts, b * n_j, lambda i, j: i * n_j + j)
    if casts:
        kern = functools.partial(kern, n_cast=len(casts))
    return pl.pallas_call(
        functools.partial(kern, tm=tm),
        grid=(b, n_j),
        in_specs=[pl.BlockSpec((None, tm, d), lambda i, j: (i, j, 0))]
        + [_const_spec(c.shape) for c in consts] + cast_in,
        out_specs=[
            pl.BlockSpec((None, tm, d), lambda i, j: (i, j, 0)),
            pl.BlockSpec((None, state_rows, d), lambda i, j: (i, 0, 0)),
        ] + cast_out,
        out_shape=[
            jax.ShapeDtypeStruct((b, s, d), F32),
            jax.ShapeDtypeStruct((b, state_rows, d), F32),
        ] + cast_shape,
        scratch_shapes=[pltpu.VMEM(shape, F32) for shape in scratch(tm, d)],
        compiler_params=_params(2),
        name=name,
    )(x, *consts, *[arr for arr, _ in casts])


def _slab(ref, j, d):
    return ref[:, j * d:(j + 1) * d]


def _pool_s_kernel(x_ref, st_ref, g_ref, w_ref, sc_ref, o_ref, nst_ref, d_ref, *, n_new, pos0):
    nb = x_ref.shape[0]
    d = g_ref.shape[1]
    grp = w_ref.shape[1]
    xs = [_slab(x_ref, l, d) for l in range(n_new)]
    hs = [_rms(x, g_ref[...]) for x in xs]

    def ext(j, c0):
        if j < POOL_BUF:
            return st_ref[:, j * d + c0:j * d + c0 + grp]
        return hs[j - POOL_BUF][:, c0:c0 + grp]

    for l in range(n_new):
        for gi, win in enumerate(POOL_WINDOWS):
            c0 = gi * grp
            acc = ext(POOL_BUF + l, c0)
            for k in range(1, win):
                acc = acc + ext(POOL_BUF + l - k, c0)
            cnt = float(min(win, pos0 + l + 1))
            d_ref[l * nb:(l + 1) * nb, c0:c0 + grp] = acc / cnt - hs[l][:, c0:c0 + grp]
    db = d_ref[...].astype(BF16)
    y = jnp.concatenate(
        [_dot(db[:, gi * grp:(gi + 1) * grp], w_ref[gi]) for gi in range(len(POOL_WINDOWS))], axis=1)
    y = y * sc_ref[...]
    for l in range(n_new):
        o_ref[l * nb:(l + 1) * nb, :] = xs[l] + y[l * nb:(l + 1) * nb, :]
    for j in range(POOL_BUF):
        src = j + n_new
        nst_ref[:, j * d:(j + 1) * d] = _slab(st_ref, src, d) if src < POOL_BUF else hs[src - POOL_BUF]


def _conv_s_kernel(x_ref, st_ref, g_ref, win_ref, bin_ref, dw_ref, dwb_ref, lng_ref, lnb_ref,
                   wout_ref, o_ref, nst_ref, g_scr, c_scr, *, n_new):
    nb = st_ref.shape[0]
    dc = wout_ref.shape[0]
    x = x_ref[...]
    h = _rms(x, g_ref[...]).astype(BF16)
    z = _dot(h, win_ref[...]) + bin_ref[...]
    g_scr[...] = z[:, :dc] * _sigmoid(z[:, dc:])

    def ext(j):
        if j < CONV_BUF:
            return _slab(st_ref, j, dc)
        return g_scr[(j - CONV_BUF) * nb:(j - CONV_BUF + 1) * nb, :]

    for l in range(n_new):
        acc = jnp.zeros((nb, dc), F32) + dwb_ref[...]
        for k in range(CONV_WIDTH):
            acc = acc + ext(l + k) * dw_ref[k:k + 1, :]
        c_scr[l * nb:(l + 1) * nb, :] = acc
    c = _silu(_ln(c_scr[...], lng_ref[...], lnb_ref[...]))
    o_ref[...] = x + _dot(c.astype(BF16), wout_ref[...])
    for j in range(CONV_BUF):
        nst_ref[:, j * dc:(j + 1) * dc] = ext(j + n_new)


def _gm_s_kernel(x_ref, g_ref, win_ref, lng_ref, lnb_ref, wrow_ref, brow_ref, wout_ref,
                 o_ref, v_ref, *, n_new, nb):
    dg = wout_ref.shape[0]
    x = x_ref[...]
    h = _rms(x, g_ref[...]).astype(BF16)
    z = _gelu_tanh(_dot(h, win_ref[...]))
    u = z[:, :dg]
    v = _ln(z[:, dg:], lng_ref[...], lnb_ref[...])
    mixed = []
    for t in range(n_new):
        acc = jnp.zeros((nb, dg), F32) + brow_ref[t:t + 1, :]
        for s in range(t + 1):
            acc = acc + wrow_ref[t * n_new + s:t * n_new + s + 1, :] * v[s * nb:(s + 1) * nb, :]
        mixed.append(acc)
    y = (u * jnp.concatenate(mixed, axis=0)).astype(BF16)
    o_ref[...] = x + _dot(y, wout_ref[...])
    for l in range(n_new):
        v_ref[:, l * dg:(l + 1) * dg] = v[l * nb:(l + 1) * nb, :]


def _sc_s_kernel(x_ref, st_ref, g_ref, win_ref, cw_ref, wout_ref, o_ref, nst_ref, *, n_new):
    nb = st_ref.shape[0]
    d = wout_ref.shape[0]
    x = x_ref[...]
    h = _rms(x, g_ref[...]).astype(BF16)
    z = _dot(h, win_ref[...])
    cx = z[:, d:2 * d] * z[:, 2 * d:]

    def ext(j):
        if j < SC_BUF:
            return _slab(st_ref, j, d)
        return cx[(j - SC_BUF) * nb:(j - SC_BUF + 1) * nb, :]

    convs = []
    for l in range(n_new):
        acc = ext(l) * cw_ref[0:1, :]
        for k in range(1, SC_WIDTH):
            acc = acc + ext(l + k) * cw_ref[k:k + 1, :]
        convs.append(acc)
    y = (z[:, :d] * jnp.concatenate(convs, axis=0)).astype(BF16)
    o_ref[...] = x + _dot(y, wout_ref[...])
    for j in range(SC_BUF):
        nst_ref[:, j * d:(j + 1) * d] = ext(j + n_new)


def _single_step(kern, operands, out_shape, scratch, name):
    return pl.pallas_call(
        kern,
        out_shape=out_shape,
        scratch_shapes=scratch,
        compiler_params=pltpu.CompilerParams(vmem_limit_bytes=VMEM_LIMIT),
        name=name,
    )(*operands)


def kernel(x_prompt, x_sample, state_pool, state_conv, state_shortconv, norm_mix, norm_ffn, norm_final, pool_w, pool_scale, conv_w_in, conv_b_in, conv_dw, conv_dw_b, conv_ln_g, conv_ln_b, conv_w_out, gm_w_in, gm_ln_g, gm_ln_b, gm_w_s, gm_b_s, gm_w_out, sc_w_in, sc_conv, sc_w_out, ffn_w_in, ffn_w_out):
    bp, sp, d = x_prompt.shape
    nb, n_new, _ = x_sample.shape
    depth = norm_mix.shape[0]
    assert depth == 4 and pool_w.shape[0] == 1, "one layer of each mixer type"
    assert sp % CHUNK == 0 and n_new < CHUNK and PAST_LEN % CHUNK == 0

    row = lambda a: a.reshape(1, -1)
    bf = lambda a: a.astype(BF16)
    gfin = row(norm_final)
    gffn = norm_ffn.reshape(depth, 1, d)
    grp = d // N_GM_GROUPS
    m = nb * n_new
    tm_shape = jax.ShapeDtypeStruct((m, d), F32)

    def ffn(x, xs, i, w_in, w_out, casts):
        out = _ffn(x.reshape(bp * sp, d), xs, gffn, w_in, w_out, gfin, casts,
                   layer=i, final=(i == depth - 1))
        return out[0].reshape(bp, sp, d), out[1], out[2:]

    pool_consts = (row(norm_mix[0]), bf(pool_w[0]), row(pool_scale[0]))
    x, pool_p, f_in, f_out = _prompt_mixer(
        functools.partial(_pool_p_kernel, pos0=0), x_prompt, pool_consts, state_rows=POOL_BUF,
        scratch=lambda tm, dd: [(POOL_HALO + tm, dd), (2, POOL_HALO + tm, dd // len(POOL_WINDOWS))],
        name="pool_p", casts=[(ffn_w_in, 0), (ffn_w_out, 0)])
    xs, pool_s = _single_step(
        functools.partial(_pool_s_kernel, n_new=n_new, pos0=PAST_LEN),
        (x_sample.reshape(nb, n_new * d), state_pool[0].reshape(nb, POOL_BUF * d)) + pool_consts,
        [tm_shape, jax.ShapeDtypeStruct((nb, POOL_BUF * d), F32)],
        [pltpu.VMEM((m, d), F32)], "pool_s")
    x, xs, (cw_in, cw_out, f_in, f_out) = ffn(
        x, xs, 0, f_in, f_out, [(conv_w_in, 0), (conv_w_out, 0), (ffn_w_in, 1), (ffn_w_out, 1)])

    conv_consts = (row(norm_mix[1]), cw_in, row(conv_b_in[0]), conv_dw[0], row(conv_dw_b[0]),
                   row(conv_ln_g[0]), row(conv_ln_b[0]), cw_out)
    x, conv_p = _conv_p(x, conv_consts)
    xs, conv_s = _single_step(
        functools.partial(_conv_s_kernel, n_new=n_new),
        (xs, state_conv[0].reshape(nb, CONV_BUF * d)) + conv_consts,
        [tm_shape, jax.ShapeDtypeStruct((nb, CONV_BUF * d), F32)],
        [pltpu.VMEM((m, d), F32), pltpu.VMEM((m, d), F32)], "conv_s")
    x, xs, (gw_in, gw_out, f_in, f_out) = ffn(
        x, xs, 1, f_in, f_out, [(gm_w_in, 0), (gm_w_out, 0), (ffn_w_in, 2), (ffn_w_out, 2)])

    gm_head = (row(norm_mix[2]), gw_in, row(gm_ln_g[0]), row(gm_ln_b[0]))
    x, v_p = _prompt_mixer(
        _gm_p_kernel, x, gm_head + (gm_w_s[0], gm_b_s[0].T, gw_out), state_rows=CHUNK,
        scratch=lambda tm, dd: [], name="gm_p")
    wrow = jnp.repeat(jnp.transpose(gm_w_s[0][:, :n_new, :n_new], (1, 2, 0)).reshape(n_new * n_new, N_GM_GROUPS), grp, axis=1)
    brow = jnp.repeat(gm_b_s[0][:, :n_new].T, grp, axis=1)
    xs, v_s = _single_step(
        functools.partial(_gm_s_kernel, n_new=n_new, nb=nb),
        (xs,) + gm_head + (wrow, brow, gw_out),
        [tm_shape, jax.ShapeDtypeStruct((nb, n_new * d), F32)],
        [], "gm_s")
    x, xs, (sw_in, sw_out, f_in, f_out) = ffn(
        x, xs, 2, f_in, f_out, [(sc_w_in, 0), (sc_w_out, 0), (ffn_w_in, 3), (ffn_w_out, 3)])

    sc_consts = (row(norm_mix[3]), sw_in, sc_conv[0], sw_out)
    x, sc_p = _prompt_mixer(
        _sc_p_kernel, x, sc_consts, state_rows=SC_BUF,
        scratch=lambda tm, dd: [(SC_HALO + tm, dd)], name="sc_p")
    xs, sc_s = _single_step(
        functools.partial(_sc_s_kernel, n_new=n_new),
        (xs, state_shortconv[0].reshape(nb, SC_BUF * d)) + sc_consts,
        [tm_shape, jax.ShapeDtypeStruct((nb, SC_BUF * d), F32)],
        [], "sc_s")
    y_prompt, ys, _ = ffn(x, xs, 3, f_in, f_out, [])
    y_sample = jnp.transpose(ys.reshape(n_new, nb, d), (1, 0, 2))

    return (y_prompt, y_sample,
            pool_p[None], pool_s.reshape(1, nb, POOL_BUF, d),
            conv_p[None], conv_s.reshape(1, nb, CONV_BUF, d),
            v_p[None], v_s.reshape(1, nb, n_new, d),
            sc_p[None], sc_s.reshape(1, nb, SC_BUF, d))
```

```python
import functools

import jax
import jax.numpy as jnp
from jax import lax
from jax.experimental import pallas as pl
from jax.experimental.pallas import tpu as pltpu

F32 = jnp.float32
BF16 = jnp.bfloat16

EPS = 1e-6
POOL_WINDOWS = (2, 4, 8, 16)
POOL_BUF = max(POOL_WINDOWS) - 1
CONV_WIDTH = 31
CONV_BUF = CONV_WIDTH - 1
CHUNK = 128
N_GM_GROUPS = 4
SC_WIDTH = 3
SC_BUF = SC_WIDTH - 1
PAST_LEN = 16384

V7X_SUBLANES = 8
V7X_BF16_SUBLANES = 16
V7X_LANES = 128
V7X_MXU_DIM = 256
V7X_VMEM_BYTES = 64 * 1024 * 1024
VMEM_LIMIT = V7X_VMEM_BYTES - 4 * 1024 * 1024

TM = 1024
POOL_HALO = 32
CONV_HALO = 32
SC_HALO = 8
CONV_UNROLL = 16
CONV_GROUP = 8


def _params(n_grid):
    return pltpu.CompilerParams(
        dimension_semantics=("arbitrary",) * n_grid, vmem_limit_bytes=VMEM_LIMIT)


def _const_spec(shape):
    nd = len(shape)
    return pl.BlockSpec(shape, lambda *_: (0,) * nd, pipeline_mode=pl.Buffered(1))


def _rms(x, g):
    y = x * lax.rsqrt(jnp.mean(x * x, axis=-1, keepdims=True) + EPS)
    return y * g


def _ln(x, g, b):
    mu = jnp.mean(x, axis=-1, keepdims=True)
    xc = x - mu
    y = xc * lax.rsqrt(jnp.mean(xc * xc, axis=-1, keepdims=True) + EPS)
    return y * g + b


def _sigmoid(x):
    return 0.5 * (1.0 + jnp.tanh(0.5 * x))


def _silu(x):
    return x * _sigmoid(x)


def _gelu_tanh(x):
    c = 0.7978845608028654
    return x * (0.5 * (1.0 + jnp.tanh(c * (x + 0.044715 * (x * x * x)))))


def _dot(a, b):
    return jnp.dot(a, b, preferred_element_type=F32)


def _ffn_chunks(d_ff, n_chunks):
    tiles = d_ff // V7X_MXU_DIM
    assert tiles * V7X_MXU_DIM == d_ff
    edges = [V7X_MXU_DIM * ((tiles * i + n_chunks - 1) // n_chunks) for i in range(n_chunks + 1)]
    return list(zip(edges[:-1], edges[1:]))


def _ffn_kernel(*refs, d_ff, chunks, final, n_cast, n_prompt_steps):
    xp_ref, xs_ref, g_ref, win_ref, wout_ref, gf_ref = refs[:6]
    cast_src = refs[6:6 + n_cast]
    op_ref, os_ref = refs[6 + n_cast:8 + n_cast]
    cast_dst = refs[8 + n_cast:]
    i = pl.program_id(0)

    def rows(x_ref, o_ref):
        x = x_ref[...]
        h = _rms(x, g_ref[...]).astype(BF16)
        acc = x
        for c0, c1 in chunks:
            gate = _dot(h, win_ref[:, c0:c1])
            up = _dot(h, win_ref[:, d_ff + c0:d_ff + c1])
            act = (_silu(gate) * up).astype(BF16)
            acc = acc + _dot(act, wout_ref[c0:c1, :])
        if final:
            acc = _rms(acc, gf_ref[...])
        o_ref[...] = acc

    @pl.when(i < n_prompt_steps)
    def _():
        rows(xp_ref, op_ref)
        for src, dst in zip(cast_src, cast_dst):
            dst[...] = src[...].astype(BF16)

    @pl.when(i == n_prompt_steps)
    def _():
        rows(xs_ref, os_ref)


def _layer_spec(shape, layer):
    nd = len(shape)
    return pl.BlockSpec((None,) + tuple(shape[1:]), lambda *_: (layer,) + (0,) * (nd - 1),
                        pipeline_mode=pl.Buffered(1))


def _cast_specs(casts, n_blocks, block_of):
    cast_in, cast_out, cast_shape = [], [], []
    for arr, idx in casts:
        _, r, c = arr.shape
        assert r % (n_blocks * V7X_BF16_SUBLANES) == 0
        cast_in.append(pl.BlockSpec((None, r // n_blocks, c),
                                    lambda *g, idx=idx: (idx, block_of(*g), 0)))
        cast_out.append(pl.BlockSpec((r // n_blocks, c), lambda *g: (block_of(*g), 0)))
        cast_shape.append(jax.ShapeDtypeStruct((r, c), BF16))
    return cast_in, cast_out, cast_shape


def _ffn(xp, xs, g, w_in, w_out, g_final, casts, *, layer, final):
    mp, d = xp.shape
    ms = xs.shape[0]
    d_ff = w_out.shape[0]
    assert mp % TM == 0
    n_steps = mp // TM
    last = n_steps - 1
    kern = functools.partial(_ffn_kernel, d_ff=d_ff, chunks=_ffn_chunks(d_ff, 2), final=final,
                             n_cast=len(casts), n_prompt_steps=n_steps)
    cast_in, cast_out, cast_shape = _cast_specs(casts, n_steps, lambda i: jnp.minimum(i, last))
    return pl.pallas_call(
        kern,
        grid=(n_steps + 1,),
        in_specs=[
            pl.BlockSpec((TM, d), lambda i: (jnp.minimum(i, last), 0)),
            _const_spec((ms, d)),
            _layer_spec(g.shape, layer),
            _const_spec(w_in.shape),
            _const_spec(w_out.shape),
            _const_spec((1, d)),
        ] + cast_in,
        out_specs=[
            pl.BlockSpec((TM, d), lambda i: (jnp.minimum(i, last), 0)),
            pl.BlockSpec((ms, d), lambda i: (0, 0)),
        ] + cast_out,
        out_shape=[jax.ShapeDtypeStruct((mp, d), F32), jax.ShapeDtypeStruct((ms, d), F32)] + cast_shape,
        compiler_params=_params(1),
        name="ffn_final" if final else "ffn",
    )(xp, xs, g, w_in, w_out, g_final, *[arr for arr, _ in casts])


def _carry_halo(ext_ref, halo, tm):
    ext_ref[0:halo, :] = ext_ref[tm:tm + halo, :]


def _pool_p_kernel(*refs, tm, pos0, n_cast):
    x_ref, g_ref, w_ref, sc_ref = refs[:4]
    cast_src = refs[4:4 + n_cast]
    o_ref, st_ref = refs[4 + n_cast:6 + n_cast]
    cast_dst = refs[6 + n_cast:6 + 2 * n_cast]
    ext_ref, tmp_ref = refs[6 + 2 * n_cast:]
    s = pl.program_id(1)
    grp = w_ref.shape[1]
    halo = POOL_HALO
    end = halo + tm

    @pl.when(s == 0)
    def _():
        ext_ref[0:halo, :] = jnp.zeros((halo, ext_ref.shape[1]), F32)

    x = x_ref[...]
    h = _rms(x, g_ref[...])
    ext_ref[halo:end, :] = h
    pos = pos0 + s * tm + lax.broadcasted_iota(jnp.int32, (tm, 1), 0)
    ys = []
    for gi, win in enumerate(POOL_WINDOWS):
        lanes = slice(gi * grp, (gi + 1) * grp)
        assert win == 1 << (win.bit_length() - 1) and V7X_SUBLANES * (win.bit_length() - 1) <= halo
        read = lambda r0, r1: ext_ref[r0:r1, lanes]
        shift, level = 1, 1
        while 2 * shift < win:
            r0 = V7X_SUBLANES * level
            slot = level % 2
            tmp_ref[slot, r0:end, :] = read(r0, end) + read(r0 - shift, end - shift)
            read = lambda r0, r1, slot=slot: tmp_ref[slot, r0:r1, :]
            shift, level = 2 * shift, level + 1
        acc = read(halo, end) + read(halo - shift, end - shift)
        inv_cnt = 1.0 / jnp.minimum(win, pos + 1).astype(F32)
        d = acc * inv_cnt - h[:, lanes]
        ys.append(_dot(d.astype(BF16), w_ref[gi]))
    y = jnp.concatenate(ys, axis=1)
    o_ref[...] = x + y * sc_ref[...]
    for src, dst in zip(cast_src, cast_dst):
        dst[...] = src[...].astype(BF16)

    @pl.when(s == pl.num_programs(1) - 1)
    def _():
        st_ref[...] = ext_ref[end - POOL_BUF:end, :]

    _carry_halo(ext_ref, halo, tm)


def _conv_p_kernel(x_ref, g_ref, win_ref, bin_ref, dw_ref, dwb_ref, lng_ref, lnb_ref, wout_ref,
                   o_ref, st_ref, glu_ref, ext_ref, c_ref, wb_ref, *, nb, ts):
    s = pl.program_id(0)
    d = x_ref.shape[2]
    n_lt, _, lane = glu_ref.shape
    pitch = glu_ref.shape[1] // nb

    @pl.when(s == 0)
    def _():
        ext_ref[:, 0:CONV_HALO * nb, :] = jnp.zeros((n_lt, CONV_HALO * nb, lane), F32)
        for j in range(n_lt):
            for k in range(CONV_WIDTH):
                wb_ref[j, k * nb:(k + 1) * nb, :] = jnp.broadcast_to(
                    dw_ref[k:k + 1, j * lane:(j + 1) * lane], (nb, lane))

    x = x_ref[...].reshape(nb * ts, d)
    h = _rms(x, g_ref[...]).astype(BF16)
    z = _dot(h, win_ref[...]) + bin_ref[...]
    dc = n_lt * lane
    glu = z[:, :dc] * _sigmoid(z[:, dc:])
    base = CONV_HALO - CONV_BUF
    for j in range(n_lt):
        lanes = slice(j * lane, (j + 1) * lane)
        for b in range(nb):
            glu_ref[j, b * pitch:b * pitch + ts, :] = glu[b * ts:(b + 1) * ts, lanes]
        for t in range(ts):
            ext_ref[j, (CONV_HALO + t) * nb:(CONV_HALO + t + 1) * nb, :] = (
                glu_ref[j, pl.ds(t, nb, stride=pitch), :])

        bias = jnp.broadcast_to(dwb_ref[:, lanes], (nb, lane))
        for t0 in range(0, ts, CONV_UNROLL):
            acc = [bias] * CONV_UNROLL
            for k0 in range(0, CONV_WIDTH, CONV_GROUP):
                n_taps = min(CONV_GROUP, CONV_WIDTH - k0)
                r0 = t0 + base + k0
                rows = [ext_ref[j, (r0 + m) * nb:(r0 + m + 1) * nb, :]
                        for m in range(CONV_UNROLL + n_taps - 1)]
                for kk in range(n_taps):
                    w = wb_ref[j, (k0 + kk) * nb:(k0 + kk + 1) * nb, :]
                    acc = [a + rows[u + kk] * w for u, a in enumerate(acc)]
            for u in range(CONV_UNROLL):
                c_ref[j, (t0 + u) * nb:(t0 + u + 1) * nb, :] = acc[u]

    c = jnp.concatenate(
        [jnp.concatenate([c_ref[j, pl.ds(b, ts, stride=nb), :] for b in range(nb)], axis=0)
         for j in range(n_lt)], axis=1)
    c = _silu(_ln(c, lng_ref[...], lnb_ref[...]))
    o_ref[...] = (x + _dot(c.astype(BF16), wout_ref[...])).reshape(nb, ts, d)

    @pl.when(s == pl.num_programs(0) - 1)
    def _():
        for b in range(nb):
            for j in range(n_lt):
                st_ref[b, :, j * lane:(j + 1) * lane] = (
                    glu_ref[j, b * pitch + ts - CONV_BUF:b * pitch + ts, :])

    ext_ref[:, 0:CONV_HALO * nb, :] = ext_ref[:, ts * nb:(ts + CONV_HALO) * nb, :]


def _conv_p(x, consts):
    nb, s, d = x.shape
    dc = consts[-1].shape[0]
    assert nb == V7X_SUBLANES, "the time-major copy puts one sequence on each sublane"
    ts = TM // nb
    assert s % ts == 0 and ts >= CONV_HALO
    return pl.pallas_call(
        functools.partial(_conv_p_kernel, nb=nb, ts=ts),
        grid=(s // ts,),
        in_specs=[pl.BlockSpec((nb, ts, d), lambda j: (0, j, 0))] + [_const_spec(c.shape) for c in consts],
        out_specs=[
            pl.BlockSpec((nb, ts, d), lambda j: (0, j, 0)),
            pl.BlockSpec((nb, CONV_BUF, dc), lambda j: (0, 0, 0)),
        ],
        out_shape=[
            jax.ShapeDtypeStruct((nb, s, d), F32),
            jax.ShapeDtypeStruct((nb, CONV_BUF, dc), F32),
        ],
        scratch_shapes=[
            pltpu.VMEM((dc // V7X_LANES, nb * (ts + V7X_SUBLANES), V7X_LANES), F32),
            pltpu.VMEM((dc // V7X_LANES, (CONV_HALO + ts) * nb, V7X_LANES), F32),
            pltpu.VMEM((dc // V7X_LANES, ts * nb, V7X_LANES), F32),
            pltpu.VMEM((dc // V7X_LANES, CONV_WIDTH * nb, V7X_LANES), F32),
        ],
        compiler_params=_params(1),
        name="conv_p",
    )(x, *consts)


def _gm_p_kernel(x_ref, g_ref, win_ref, lng_ref, lnb_ref, ws_ref, bs_ref, wout_ref,
                 o_ref, v_ref, *, tm):
    s = pl.program_id(1)
    dg = wout_ref.shape[0]
    grp = dg // N_GM_GROUPS
    row = lax.broadcasted_iota(jnp.int32, (CHUNK, CHUNK), 0)
    col = lax.broadcasted_iota(jnp.int32, (CHUNK, CHUNK), 1)
    causal = col <= row
    x = x_ref[...]
    h = _rms(x, g_ref[...]).astype(BF16)
    z = _gelu_tanh(_dot(h, win_ref[...]))
    u = z[:, :dg]
    v = _ln(z[:, dg:], lng_ref[...], lnb_ref[...])
    vb = v.astype(BF16)
    mixed_cols = []
    for gi in range(N_GM_GROUPS):
        wsg = jnp.where(causal, ws_ref[gi], 0.0).astype(BF16)
        bias = bs_ref[:, gi:gi + 1]
        parts = [_dot(wsg, vb[c0:c0 + CHUNK, gi * grp:(gi + 1) * grp]) + bias
                 for c0 in range(0, tm, CHUNK)]
        mixed_cols.append(jnp.concatenate(parts, axis=0))
    mixed = jnp.concatenate(mixed_cols, axis=1)
    y = (u * mixed).astype(BF16)
    o_ref[...] = x + _dot(y, wout_ref[...])

    @pl.when(s == pl.num_programs(1) - 1)
    def _():
        v_ref[...] = v[tm - CHUNK:, :]


def _sc_p_kernel(x_ref, g_ref, win_ref, cw_ref, wout_ref, o_ref, st_ref, ext_ref, *, tm):
    s = pl.program_id(1)
    d = wout_ref.shape[0]

    @pl.when(s == 0)
    def _():
        ext_ref[0:SC_HALO, :] = jnp.zeros((SC_HALO, d), F32)

    x = x_ref[...]
    h = _rms(x, g_ref[...]).astype(BF16)
    z = _dot(h, win_ref[...])
    cx = z[:, d:2 * d] * z[:, 2 * d:]
    ext_ref[SC_HALO:SC_HALO + tm, :] = cx
    conv = cx * cw_ref[SC_WIDTH - 1:SC_WIDTH, :]
    for k in range(SC_WIDTH - 1):
        off = SC_HALO - (SC_WIDTH - 1) + k
        conv = conv + ext_ref[off:off + tm, :] * cw_ref[k:k + 1, :]
    y = (z[:, :d] * conv).astype(BF16)
    o_ref[...] = x + _dot(y, wout_ref[...])

    @pl.when(s == pl.num_programs(1) - 1)
    def _():
        st_ref[...] = ext_ref[SC_HALO + tm - SC_BUF:SC_HALO + tm, :]

    _carry_halo(ext_ref, SC_HALO, tm)


def _prompt_mixer(kern, x, consts, *, state_rows, scratch, name, casts=()):
    b, s, d = x.shape
    tm = min(TM, s)
    n_j = s // tm
    cast_in, cast_out, cast_shape = _cast_specs(casts, b * n_j, lambda i, j: i * n_j + j)
    if casts:
        kern = functools.partial(kern, n_cast=len(casts))
    return pl.pallas_call(
        functools.partial(kern, tm=tm),
        grid=(b, n_j),
        in_specs=[pl.BlockSpec((None, tm, d), lambda i, j: (i, j, 0))]
        + [_const_spec(c.shape) for c in consts] + cast_in,
        out_specs=[
            pl.BlockSpec((None, tm, d), lambda i, j: (i, j, 0)),
            pl.BlockSpec((None, state_rows, d), lambda i, j: (i, 0, 0)),
        ] + cast_out,
        out_shape=[
            jax.ShapeDtypeStruct((b, s, d), F32),
            jax.ShapeDtypeStruct((b, state_rows, d), F32),
        ] + cast_shape,
        scratch_shapes=[pltpu.VMEM(shape, F32) for shape in scratch(tm, d)],
        compiler_params=_params(2),
        name=name,
    )(x, *consts, *[arr for arr, _ in casts])


def _slab(ref, j, d):
    return ref[:, j * d:(j + 1) * d]


def _pool_s_kernel(x_ref, st_ref, g_ref, w_ref, sc_ref, o_ref, nst_ref, d_ref, *, n_new, pos0):
    nb = x_ref.shape[0]
    grp = w_ref.shape[1]
    xs = [x_ref[:, l, :] for l in range(n_new)]
    hs = [_rms(x, g_ref[...]) for x in xs]

    def ext(j, c0):
        if j < POOL_BUF:
            return st_ref[j, :, c0:c0 + grp]
        return hs[j - POOL_BUF][:, c0:c0 + grp]

    for l in range(n_new):
        for gi, win in enumerate(POOL_WINDOWS):
            c0 = gi * grp
            acc = ext(POOL_BUF + l, c0)
            for k in range(1, win):
                acc = acc + ext(POOL_BUF + l - k, c0)
            cnt = float(min(win, pos0 + l + 1))
            d_ref[l * nb:(l + 1) * nb, c0:c0 + grp] = acc / cnt - hs[l][:, c0:c0 + grp]
    db = d_ref[...].astype(BF16)
    y = jnp.concatenate(
        [_dot(db[:, gi * grp:(gi + 1) * grp], w_ref[gi]) for gi in range(len(POOL_WINDOWS))], axis=1)
    y = y * sc_ref[...]
    for l in range(n_new):
        o_ref[l * nb:(l + 1) * nb, :] = xs[l] + y[l * nb:(l + 1) * nb, :]
    for j in range(POOL_BUF):
        src = j + n_new
        nst_ref[j] = st_ref[src] if src < POOL_BUF else hs[src - POOL_BUF]


def _conv_s_kernel(x_ref, st_ref, g_ref, win_ref, bin_ref, dw_ref, dwb_ref, lng_ref, lnb_ref,
                   wout_ref, o_ref, nst_ref, g_scr, c_scr, *, n_new):
    nb = st_ref.shape[1]
    dc = wout_ref.shape[0]
    x = x_ref[...]
    h = _rms(x, g_ref[...]).astype(BF16)
    z = _dot(h, win_ref[...]) + bin_ref[...]
    g_scr[...] = z[:, :dc] * _sigmoid(z[:, dc:])

    def ext(j):
        if j < CONV_BUF:
            return st_ref[j]
        return g_scr[(j - CONV_BUF) * nb:(j - CONV_BUF + 1) * nb, :]

    for l in range(n_new):
        acc = jnp.zeros((nb, dc), F32) + dwb_ref[...]
        for k in range(CONV_WIDTH):
            acc = acc + ext(l + k) * dw_ref[k:k + 1, :]
        c_scr[l * nb:(l + 1) * nb, :] = acc
    c = _silu(_ln(c_scr[...], lng_ref[...], lnb_ref[...]))
    o_ref[...] = x + _dot(c.astype(BF16), wout_ref[...])
    for j in range(CONV_BUF):
        nst_ref[j] = ext(j + n_new)


def _gm_s_kernel(x_ref, g_ref, win_ref, lng_ref, lnb_ref, wrow_ref, brow_ref, wout_ref,
                 o_ref, v_ref, *, n_new, nb):
    dg = wout_ref.shape[0]
    x = x_ref[...]
    h = _rms(x, g_ref[...]).astype(BF16)
    z = _gelu_tanh(_dot(h, win_ref[...]))
    u = z[:, :dg]
    v = _ln(z[:, dg:], lng_ref[...], lnb_ref[...])
    mixed = []
    for t in range(n_new):
        acc = jnp.zeros((nb, dg), F32) + brow_ref[t:t + 1, :]
        for s in range(t + 1):
            acc = acc + wrow_ref[t * n_new + s:t * n_new + s + 1, :] * v[s * nb:(s + 1) * nb, :]
        mixed.append(acc)
    y = (u * jnp.concatenate(mixed, axis=0)).astype(BF16)
    o_ref[...] = x + _dot(y, wout_ref[...])
    for l in range(n_new):
        v_ref[:, l * dg:(l + 1) * dg] = v[l * nb:(l + 1) * nb, :]


def _sc_s_kernel(x_ref, st_ref, g_ref, win_ref, cw_ref, wout_ref, o_ref, nst_ref, *, n_new):
    nb = st_ref.shape[0]
    d = wout_ref.shape[0]
    x = x_ref[...]
    h = _rms(x, g_ref[...]).astype(BF16)
    z = _dot(h, win_ref[...])
    cx = z[:, d:2 * d] * z[:, 2 * d:]

    def ext(j):
        if j < SC_BUF:
            return _slab(st_ref, j, d)
        return cx[(j - SC_BUF) * nb:(j - SC_BUF + 1) * nb, :]

    convs = []
    for l in range(n_new):
        acc = ext(l) * cw_ref[0:1, :]
        for k in range(1, SC_WIDTH):
            acc = acc + ext(l + k) * cw_ref[k:k + 1, :]
        convs.append(acc)
    y = (z[:, :d] * jnp.concatenate(convs, axis=0)).astype(BF16)
    o_ref[...] = x + _dot(y, wout_ref[...])
    for j in range(SC_BUF):
        nst_ref[:, j * d:(j + 1) * d] = ext(j + n_new)


def _single_step(kern, operands, out_shape, scratch, name):
    return pl.pallas_call(
        kern,
        out_shape=out_shape,
        scratch_shapes=scratch,
        compiler_params=pltpu.CompilerParams(vmem_limit_bytes=VMEM_LIMIT),
        name=name,
    )(*operands)


def kernel(x_prompt, x_sample, state_pool, state_conv, state_shortconv, norm_mix, norm_ffn, norm_final, pool_w, pool_scale, conv_w_in, conv_b_in, conv_dw, conv_dw_b, conv_ln_g, conv_ln_b, conv_w_out, gm_w_in, gm_ln_g, gm_ln_b, gm_w_s, gm_b_s, gm_w_out, sc_w_in, sc_conv, sc_w_out, ffn_w_in, ffn_w_out):
    bp, sp, d = x_prompt.shape
    nb, n_new, _ = x_sample.shape
    depth = norm_mix.shape[0]
    assert depth == 4 and pool_w.shape[0] == 1, "one layer of each mixer type"
    assert sp % CHUNK == 0 and n_new < CHUNK and PAST_LEN % CHUNK == 0

    row = lambda a: a.reshape(1, -1)
    bf = lambda a: a.astype(BF16)
    rows_major = lambda a: jnp.transpose(a, (1, 0, 2))
    gfin = row(norm_final)
    gffn = norm_ffn.reshape(depth, 1, d)
    grp = d // N_GM_GROUPS
    m = nb * n_new
    tm_shape = jax.ShapeDtypeStruct((m, d), F32)

    def ffn(x, xs, i, w_in, w_out, casts):
        out = _ffn(x.reshape(bp * sp, d), xs, gffn, w_in, w_out, gfin, casts,
                   layer=i, final=(i == depth - 1))
        return out[0].reshape(bp, sp, d), out[1], out[2:]

    pool_consts = (row(norm_mix[0]), bf(pool_w[0]), row(pool_scale[0]))
    x, pool_p, f_in, f_out = _prompt_mixer(
        functools.partial(_pool_p_kernel, pos0=0), x_prompt, pool_consts, state_rows=POOL_BUF,
        scratch=lambda tm, dd: [(POOL_HALO + tm, dd), (2, POOL_HALO + tm, dd // len(POOL_WINDOWS))],
        name="pool_p", casts=[(ffn_w_in, 0), (ffn_w_out, 0)])
    xs, pool_s = _single_step(
        functools.partial(_pool_s_kernel, n_new=n_new, pos0=PAST_LEN),
        (x_sample, rows_major(state_pool[0])) + pool_consts,
        [tm_shape, jax.ShapeDtypeStruct((POOL_BUF, nb, d), F32)],
        [pltpu.VMEM((m, d), F32)], "pool_s")
    x, xs, (cw_in, cw_out, f_in, f_out) = ffn(
        x, xs, 0, f_in, f_out, [(conv_w_in, 0), (conv_w_out, 0), (ffn_w_in, 1), (ffn_w_out, 1)])

    conv_consts = (row(norm_mix[1]), cw_in, row(conv_b_in[0]), conv_dw[0], row(conv_dw_b[0]),
                   row(conv_ln_g[0]), row(conv_ln_b[0]), cw_out)
    x, conv_p = _conv_p(x, conv_consts)
    xs, conv_s = _single_step(
        functools.partial(_conv_s_kernel, n_new=n_new),
        (xs, rows_major(state_conv[0])) + conv_consts,
        [tm_shape, jax.ShapeDtypeStruct((CONV_BUF, nb, d), F32)],
        [pltpu.VMEM((m, d), F32), pltpu.VMEM((m, d), F32)], "conv_s")
    x, xs, (gw_in, gw_out, f_in, f_out) = ffn(
        x, xs, 1, f_in, f_out, [(gm_w_in, 0), (gm_w_out, 0), (ffn_w_in, 2), (ffn_w_out, 2)])

    gm_head = (row(norm_mix[2]), gw_in, row(gm_ln_g[0]), row(gm_ln_b[0]))
    x, v_p = _prompt_mixer(
        _gm_p_kernel, x, gm_head + (gm_w_s[0], gm_b_s[0].T, gw_out), state_rows=CHUNK,
        scratch=lambda tm, dd: [], name="gm_p")
    wrow = jnp.repeat(jnp.transpose(gm_w_s[0][:, :n_new, :n_new], (1, 2, 0)).reshape(n_new * n_new, N_GM_GROUPS), grp, axis=1)
    brow = jnp.repeat(gm_b_s[0][:, :n_new].T, grp, axis=1)
    xs, v_s = _single_step(
        functools.partial(_gm_s_kernel, n_new=n_new, nb=nb),
        (xs,) + gm_head + (wrow, brow, gw_out),
        [tm_shape, jax.ShapeDtypeStruct((nb, n_new * d), F32)],
        [], "gm_s")
    x, xs, (sw_in, sw_out, f_in, f_out) = ffn(
        x, xs, 2, f_in, f_out, [(sc_w_in, 0), (sc_w_out, 0), (ffn_w_in, 3), (ffn_w_out, 3)])

    sc_consts = (row(norm_mix[3]), sw_in, sc_conv[0], sw_out)
    x, sc_p = _prompt_mixer(
        _sc_p_kernel, x, sc_consts, state_rows=SC_BUF,
        scratch=lambda tm, dd: [(SC_HALO + tm, dd)], name="sc_p")
    xs, sc_s = _single_step(
        functools.partial(_sc_s_kernel, n_new=n_new),
        (xs, state_shortconv[0].reshape(nb, SC_BUF * d)) + sc_consts,
        [tm_shape, jax.ShapeDtypeStruct((nb, SC_BUF * d), F32)],
        [], "sc_s")
    y_prompt, ys, _ = ffn(x, xs, 3, f_in, f_out, [])
    y_sample = jnp.transpose(ys.reshape(n_new, nb, d), (1, 0, 2))

    return (y_prompt, y_sample,
            pool_p[None], rows_major(pool_s)[None],
            conv_p[None], rows_major(conv_s)[None],
            v_p[None], v_s.reshape(1, nb, n_new, d),
            sc_p[None], sc_s.reshape(1, nb, SC_BUF, d))
```

```python
import functools

import jax
import jax.numpy as jnp
from jax import lax
from jax.experimental import pallas as pl
from jax.experimental.pallas import tpu as pltpu

F32 = jnp.float32
BF16 = jnp.bfloat16

EPS = 1e-6
POOL_WINDOWS = (2, 4, 8, 16)
POOL_BUF = max(POOL_WINDOWS) - 1
CONV_WIDTH = 31
CONV_BUF = CONV_WIDTH - 1
CHUNK = 128
N_GM_GROUPS = 4
SC_WIDTH = 3
SC_BUF = SC_WIDTH - 1
PAST_LEN = 16384

V7X_SUBLANES = 8
V7X_BF16_SUBLANES = 16
V7X_LANES = 128
V7X_MXU_DIM = 256
V7X_VMEM_BYTES = 64 * 1024 * 1024
VMEM_LIMIT = V7X_VMEM_BYTES - 4 * 1024 * 1024

TM = 1024
POOL_HALO = 32
CONV_HALO = 32
SC_HALO = 8
CONV_UNROLL = 16
CONV_GROUP = 8


def _params(n_grid):
    return pltpu.CompilerParams(
        dimension_semantics=("arbitrary",) * n_grid, vmem_limit_bytes=VMEM_LIMIT)


def _const_spec(shape):
    nd = len(shape)
    return pl.BlockSpec(shape, lambda *_: (0,) * nd, pipeline_mode=pl.Buffered(1))


def _rms(x, g):
    y = x * lax.rsqrt(jnp.mean(x * x, axis=-1, keepdims=True) + EPS)
    return y * g


def _ln(x, g, b):
    mu = jnp.mean(x, axis=-1, keepdims=True)
    xc = x - mu
    y = xc * lax.rsqrt(jnp.mean(xc * xc, axis=-1, keepdims=True) + EPS)
    return y * g + b


def _sigmoid(x):
    return 0.5 * (1.0 + jnp.tanh(0.5 * x))


def _silu(x):
    return x * _sigmoid(x)


def _gelu_tanh(x):
    c = 0.7978845608028654
    return x * (0.5 * (1.0 + jnp.tanh(c * (x + 0.044715 * (x * x * x)))))


def _dot(a, b):
    return jnp.dot(a, b, preferred_element_type=F32)


def _ffn_chunks(d_ff, n_chunks):
    tiles = d_ff // V7X_MXU_DIM
    assert tiles * V7X_MXU_DIM == d_ff
    edges = [V7X_MXU_DIM * ((tiles * i + n_chunks - 1) // n_chunks) for i in range(n_chunks + 1)]
    return list(zip(edges[:-1], edges[1:]))


def _ffn_kernel(*refs, d_ff, chunks, final, n_cast, n_prompt_steps):
    xp_ref, xs_ref, g_ref, win_ref, wout_ref, gf_ref = refs[:6]
    cast_src = refs[6:6 + n_cast]
    op_ref, os_ref = refs[6 + n_cast:8 + n_cast]
    cast_dst = refs[8 + n_cast:]
    i = pl.program_id(0)

    def rows(x_ref, o_ref):
        x = x_ref[...]
        h = _rms(x, g_ref[...]).astype(BF16)
        acc = x
        for c0, c1 in chunks:
            gate = _dot(h, win_ref[:, c0:c1])
            up = _dot(h, win_ref[:, d_ff + c0:d_ff + c1])
            act = (_silu(gate) * up).astype(BF16)
            acc = acc + _dot(act, wout_ref[c0:c1, :])
        if final:
            acc = _rms(acc, gf_ref[...])
        o_ref[...] = acc

    @pl.when(i < n_prompt_steps)
    def _():
        rows(xp_ref, op_ref)
        for src, dst in zip(cast_src, cast_dst):
            dst[...] = src[...].astype(BF16)

    @pl.when(i == n_prompt_steps)
    def _():
        rows(xs_ref, os_ref)


def _layer_spec(shape, layer):
    nd = len(shape)
    return pl.BlockSpec((None,) + tuple(shape[1:]), lambda *_: (layer,) + (0,) * (nd - 1),
                        pipeline_mode=pl.Buffered(1))


def _cast_specs(casts, n_blocks, block_of):
    cast_in, cast_out, cast_shape = [], [], []
    for arr, idx in casts:
        _, r, c = arr.shape
        assert r % (n_blocks * V7X_BF16_SUBLANES) == 0
        cast_in.append(pl.BlockSpec((None, r // n_blocks, c),
                                    lambda *g, idx=idx: (idx, block_of(*g), 0)))
        cast_out.append(pl.BlockSpec((r // n_blocks, c), lambda *g: (block_of(*g), 0)))
        cast_shape.append(jax.ShapeDtypeStruct((r, c), BF16))
    return cast_in, cast_out, cast_shape


def _ffn(xp, xs, g, w_in, w_out, g_final, casts, *, layer, final):
    mp, d = xp.shape
    ms = xs.shape[0]
    d_ff = w_out.shape[0]
    assert mp % TM == 0
    n_steps = mp // TM
    last = n_steps - 1
    kern = functools.partial(_ffn_kernel, d_ff=d_ff, chunks=_ffn_chunks(d_ff, 2), final=final,
                             n_cast=len(casts), n_prompt_steps=n_steps)
    cast_in, cast_out, cast_shape = _cast_specs(casts, n_steps, lambda i: jnp.minimum(i, last))
    return pl.pallas_call(
        kern,
        grid=(n_steps + 1,),
        in_specs=[
            pl.BlockSpec((TM, d), lambda i: (jnp.minimum(i, last), 0)),
            _const_spec((ms, d)),
            _layer_spec(g.shape, layer),
            _const_spec(w_in.shape),
            _const_spec(w_out.shape),
            _const_spec((1, d)),
        ] + cast_in,
        out_specs=[
            pl.BlockSpec((TM, d), lambda i: (jnp.minimum(i, last), 0)),
            pl.BlockSpec((ms, d), lambda i: (0, 0)),
        ] + cast_out,
        out_shape=[jax.ShapeDtypeStruct((mp, d), F32), jax.ShapeDtypeStruct((ms, d), F32)] + cast_shape,
        compiler_params=_params(1),
        name="ffn_final" if final else "ffn",
    )(xp, xs, g, w_in, w_out, g_final, *[arr for arr, _ in casts])


def _carry_halo(ext_ref, halo, tm):
    ext_ref[0:halo, :] = ext_ref[tm:tm + halo, :]


def _pool_p_kernel(*refs, tm, pos0, n_cast):
    x_ref, g_ref, w_ref, sc_ref = refs[:4]
    cast_src = refs[4:4 + n_cast]
    o_ref, st_ref = refs[4 + n_cast:6 + n_cast]
    cast_dst = refs[6 + n_cast:6 + 2 * n_cast]
    ext_ref, tmp_ref = refs[6 + 2 * n_cast:]
    s = pl.program_id(1)
    grp = w_ref.shape[1]
    halo = POOL_HALO
    end = halo + tm

    @pl.when(s == 0)
    def _():
        ext_ref[0:halo, :] = jnp.zeros((halo, ext_ref.shape[1]), F32)

    x = x_ref[...]
    h = _rms(x, g_ref[...])
    ext_ref[halo:end, :] = h
    pos = pos0 + s * tm + lax.broadcasted_iota(jnp.int32, (tm, 1), 0)
    ys = []
    for gi, win in enumerate(POOL_WINDOWS):
        lanes = slice(gi * grp, (gi + 1) * grp)
        assert win == 1 << (win.bit_length() - 1) and V7X_SUBLANES * (win.bit_length() - 1) <= halo
        read = lambda r0, r1: ext_ref[r0:r1, lanes]
        shift, level = 1, 1
        while 2 * shift < win:
            r0 = V7X_SUBLANES * level
            slot = level % 2
            tmp_ref[slot, r0:end, :] = read(r0, end) + read(r0 - shift, end - shift)
            read = lambda r0, r1, slot=slot: tmp_ref[slot, r0:r1, :]
            shift, level = 2 * shift, level + 1
        acc = read(halo, end) + read(halo - shift, end - shift)
        inv_cnt = 1.0 / jnp.minimum(win, pos + 1).astype(F32)
        d = acc * inv_cnt - h[:, lanes]
        ys.append(_dot(d.astype(BF16), w_ref[gi]))
    y = jnp.concatenate(ys, axis=1)
    o_ref[...] = x + y * sc_ref[...]
    for src, dst in zip(cast_src, cast_dst):
        dst[...] = src[...].astype(BF16)

    @pl.when(s == pl.num_programs(1) - 1)
    def _():
        st_ref[...] = ext_ref[end - POOL_BUF:end, :]

    _carry_halo(ext_ref, halo, tm)


def _conv_p_kernel(x_ref, g_ref, win_ref, bin_ref, dw_ref, dwb_ref, lng_ref, lnb_ref, wout_ref,
                   o_ref, st_ref, glu_ref, ext_ref, c_ref, wb_ref, *, nb, ts):
    s = pl.program_id(0)
    d = x_ref.shape[2]
    n_lt, _, lane = glu_ref.shape
    pitch = glu_ref.shape[1] // nb

    @pl.when(s == 0)
    def _():
        ext_ref[:, 0:CONV_HALO * nb, :] = jnp.zeros((n_lt, CONV_HALO * nb, lane), F32)
        for j in range(n_lt):
            for k in range(CONV_WIDTH):
                wb_ref[j, k * nb:(k + 1) * nb, :] = jnp.broadcast_to(
                    dw_ref[k:k + 1, j * lane:(j + 1) * lane], (nb, lane))

    x = x_ref[...].reshape(nb * ts, d)
    h = _rms(x, g_ref[...]).astype(BF16)
    z = _dot(h, win_ref[...]) + bin_ref[...]
    dc = n_lt * lane
    glu = z[:, :dc] * _sigmoid(z[:, dc:])
    base = CONV_HALO - CONV_BUF
    for j in range(n_lt):
        lanes = slice(j * lane, (j + 1) * lane)
        for b in range(nb):
            glu_ref[j, b * pitch:b * pitch + ts, :] = glu[b * ts:(b + 1) * ts, lanes]
        for t in range(ts):
            ext_ref[j, (CONV_HALO + t) * nb:(CONV_HALO + t + 1) * nb, :] = (
                glu_ref[j, pl.ds(t, nb, stride=pitch), :])

        bias = jnp.broadcast_to(dwb_ref[:, lanes], (nb, lane))
        for t0 in range(0, ts, CONV_UNROLL):
            acc = [bias] * CONV_UNROLL
            for k0 in range(0, CONV_WIDTH, CONV_GROUP):
                n_taps = min(CONV_GROUP, CONV_WIDTH - k0)
                r0 = t0 + base + k0
                rows = [ext_ref[j, (r0 + m) * nb:(r0 + m + 1) * nb, :]
                        for m in range(CONV_UNROLL + n_taps - 1)]
                for kk in range(n_taps):
                    w = wb_ref[j, (k0 + kk) * nb:(k0 + kk + 1) * nb, :]
                    acc = [a + rows[u + kk] * w for u, a in enumerate(acc)]
            for u in range(CONV_UNROLL):
                c_ref[j, (t0 + u) * nb:(t0 + u + 1) * nb, :] = acc[u]

    c = jnp.concatenate(
        [jnp.concatenate([c_ref[j, pl.ds(b, ts, stride=nb), :] for b in range(nb)], axis=0)
         for j in range(n_lt)], axis=1)
    c = _silu(_ln(c, lng_ref[...], lnb_ref[...]))
    o_ref[...] = (x + _dot(c.astype(BF16), wout_ref[...])).reshape(nb, ts, d)

    @pl.when(s == pl.num_programs(0) - 1)
    def _():
        for b in range(nb):
            for j in range(n_lt):
                st_ref[b, :, j * lane:(j + 1) * lane] = (
                    glu_ref[j, b * pitch + ts - CONV_BUF:b * pitch + ts, :])

    ext_ref[:, 0:CONV_HALO * nb, :] = ext_ref[:, ts * nb:(ts + CONV_HALO) * nb, :]


def _conv_p(x, consts):
    nb, s, d = x.shape
    dc = consts[-1].shape[0]
    assert nb == V7X_SUBLANES, "the time-major copy puts one sequence on each sublane"
    ts = TM // nb
    assert s % ts == 0 and ts >= CONV_HALO
    return pl.pallas_call(
        functools.partial(_conv_p_kernel, nb=nb, ts=ts),
        grid=(s // ts,),
        in_specs=[pl.BlockSpec((nb, ts, d), lambda j: (0, j, 0))] + [_const_spec(c.shape) for c in consts],
        out_specs=[
            pl.BlockSpec((nb, ts, d), lambda j: (0, j, 0)),
            pl.BlockSpec((nb, CONV_BUF, dc), lambda j: (0, 0, 0)),
        ],
        out_shape=[
            jax.ShapeDtypeStruct((nb, s, d), F32),
            jax.ShapeDtypeStruct((nb, CONV_BUF, dc), F32),
        ],
        scratch_shapes=[
            pltpu.VMEM((dc // V7X_LANES, nb * (ts + V7X_SUBLANES), V7X_LANES), F32),
            pltpu.VMEM((dc // V7X_LANES, (CONV_HALO + ts) * nb, V7X_LANES), F32),
            pltpu.VMEM((dc // V7X_LANES, ts * nb, V7X_LANES), F32),
            pltpu.VMEM((dc // V7X_LANES, CONV_WIDTH * nb, V7X_LANES), F32),
        ],
        compiler_params=_params(1),
        name="conv_p",
    )(x, *consts)


def _gm_p_kernel(x_ref, g_ref, win_ref, lng_ref, lnb_ref, ws_ref, bs_ref, wout_ref,
                 o_ref, v_ref, *, tm):
    s = pl.program_id(1)
    dg = wout_ref.shape[0]
    grp = dg // N_GM_GROUPS
    row = lax.broadcasted_iota(jnp.int32, (CHUNK, CHUNK), 0)
    col = lax.broadcasted_iota(jnp.int32, (CHUNK, CHUNK), 1)
    causal = col <= row
    x = x_ref[...]
    h = _rms(x, g_ref[...]).astype(BF16)
    z = _gelu_tanh(_dot(h, win_ref[...]))
    u = z[:, :dg]
    v = _ln(z[:, dg:], lng_ref[...], lnb_ref[...])
    vb = v.astype(BF16)
    mixed_cols = []
    for gi in range(N_GM_GROUPS):
        wsg = jnp.where(causal, ws_ref[gi], 0.0).astype(BF16)
        bias = bs_ref[:, gi:gi + 1]
        parts = [_dot(wsg, vb[c0:c0 + CHUNK, gi * grp:(gi + 1) * grp]) + bias
                 for c0 in range(0, tm, CHUNK)]
        mixed_cols.append(jnp.concatenate(parts, axis=0))
    mixed = jnp.concatenate(mixed_cols, axis=1)
    y = (u * mixed).astype(BF16)
    o_ref[...] = x + _dot(y, wout_ref[...])

    @pl.when(s == pl.num_programs(1) - 1)
    def _():
        v_ref[...] = v[tm - CHUNK:, :]


def _sc_p_kernel(x_ref, g_ref, win_ref, cw_ref, wout_ref, o_ref, st_ref, ext_ref, *, tm):
    s = pl.program_id(1)
    d = wout_ref.shape[0]

    @pl.when(s == 0)
    def _():
        ext_ref[0:SC_HALO, :] = jnp.zeros((SC_HALO, d), F32)

    x = x_ref[...]
    h = _rms(x, g_ref[...]).astype(BF16)
    z = _dot(h, win_ref[...])
    cx = z[:, d:2 * d] * z[:, 2 * d:]
    ext_ref[SC_HALO:SC_HALO + tm, :] = cx
    conv = cx * cw_ref[SC_WIDTH - 1:SC_WIDTH, :]
    for k in range(SC_WIDTH - 1):
        off = SC_HALO - (SC_WIDTH - 1) + k
        conv = conv + ext_ref[off:off + tm, :] * cw_ref[k:k + 1, :]
    y = (z[:, :d] * conv).astype(BF16)
    o_ref[...] = x + _dot(y, wout_ref[...])

    @pl.when(s == pl.num_programs(1) - 1)
    def _():
        st_ref[...] = ext_ref[SC_HALO + tm - SC_BUF:SC_HALO + tm, :]

    _carry_halo(ext_ref, SC_HALO, tm)


def _prompt_mixer(kern, x, consts, *, state_rows, scratch, name, casts=()):
    b, s, d = x.shape
    tm = min(TM, s)
    n_j = s // tm
    cast_in, cast_out, cast_shape = _cast_specs(casts, b * n_j, lambda i, j: i * n_j + j)
    if casts:
        kern = functools.partial(kern, n_cast=len(casts))
    return pl.pallas_call(
        functools.partial(kern, tm=tm),
        grid=(b, n_j),
        in_specs=[pl.BlockSpec((None, tm, d), lambda i, j: (i, j, 0))]
        + [_const_spec(c.shape) for c in consts] + cast_in,
        out_specs=[
            pl.BlockSpec((None, tm, d), lambda i, j: (i, j, 0)),
            pl.BlockSpec((None, state_rows, d), lambda i, j: (i, 0, 0)),
        ] + cast_out,
        out_shape=[
            jax.ShapeDtypeStruct((b, s, d), F32),
            jax.ShapeDtypeStruct((b, state_rows, d), F32),
        ] + cast_shape,
        scratch_shapes=[pltpu.VMEM(shape, F32) for shape in scratch(tm, d)],
        compiler_params=_params(2),
        name=name,
    )(x, *consts, *[arr for arr, _ in casts])


def _slab(ref, j, d):
    return ref[:, j * d:(j + 1) * d]


def _pool_s_kernel(x_ref, st_ref, g_ref, w_ref, sc_ref, o_ref, nst_ref, d_ref, *, n_new, pos0):
    nb = x_ref.shape[1]
    grp = w_ref.shape[1]
    xs = [x_ref[l] for l in range(n_new)]
    hs = [_rms(x, g_ref[...]) for x in xs]

    def ext(j, c0):
        if j < POOL_BUF:
            return st_ref[j, :, c0:c0 + grp]
        return hs[j - POOL_BUF][:, c0:c0 + grp]

    for l in range(n_new):
        for gi, win in enumerate(POOL_WINDOWS):
            c0 = gi * grp
            acc = ext(POOL_BUF + l, c0)
            for k in range(1, win):
                acc = acc + ext(POOL_BUF + l - k, c0)
            cnt = float(min(win, pos0 + l + 1))
            d_ref[l * nb:(l + 1) * nb, c0:c0 + grp] = acc / cnt - hs[l][:, c0:c0 + grp]
    db = d_ref[...].astype(BF16)
    y = jnp.concatenate(
        [_dot(db[:, gi * grp:(gi + 1) * grp], w_ref[gi]) for gi in range(len(POOL_WINDOWS))], axis=1)
    y = y * sc_ref[...]
    for l in range(n_new):
        o_ref[l * nb:(l + 1) * nb, :] = xs[l] + y[l * nb:(l + 1) * nb, :]
    for j in range(POOL_BUF):
        src = j + n_new
        nst_ref[j] = st_ref[src] if src < POOL_BUF else hs[src - POOL_BUF]


def _conv_s_kernel(x_ref, st_ref, g_ref, win_ref, bin_ref, dw_ref, dwb_ref, lng_ref, lnb_ref,
                   wout_ref, o_ref, nst_ref, g_scr, c_scr, *, n_new):
    nb = st_ref.shape[1]
    dc = wout_ref.shape[0]
    x = x_ref[...]
    h = _rms(x, g_ref[...]).astype(BF16)
    z = _dot(h, win_ref[...]) + bin_ref[...]
    g_scr[...] = z[:, :dc] * _sigmoid(z[:, dc:])

    def ext(j):
        if j < CONV_BUF:
            return st_ref[j]
        return g_scr[(j - CONV_BUF) * nb:(j - CONV_BUF + 1) * nb, :]

    for l in range(n_new):
        acc = jnp.zeros((nb, dc), F32) + dwb_ref[...]
        for k in range(CONV_WIDTH):
            acc = acc + ext(l + k) * dw_ref[k:k + 1, :]
        c_scr[l * nb:(l + 1) * nb, :] = acc
    c = _silu(_ln(c_scr[...], lng_ref[...], lnb_ref[...]))
    o_ref[...] = x + _dot(c.astype(BF16), wout_ref[...])
    for j in range(CONV_BUF):
        nst_ref[j] = ext(j + n_new)


def _gm_s_kernel(x_ref, g_ref, win_ref, lng_ref, lnb_ref, wrow_ref, brow_ref, wout_ref,
                 o_ref, v_ref, *, n_new, nb):
    dg = wout_ref.shape[0]
    x = x_ref[...]
    h = _rms(x, g_ref[...]).astype(BF16)
    z = _gelu_tanh(_dot(h, win_ref[...]))
    u = z[:, :dg]
    v = _ln(z[:, dg:], lng_ref[...], lnb_ref[...])
    mixed = []
    for t in range(n_new):
        acc = jnp.zeros((nb, dg), F32) + brow_ref[t:t + 1, :]
        for s in range(t + 1):
            acc = acc + wrow_ref[t * n_new + s:t * n_new + s + 1, :] * v[s * nb:(s + 1) * nb, :]
        mixed.append(acc)
    y = (u * jnp.concatenate(mixed, axis=0)).astype(BF16)
    o_ref[...] = x + _dot(y, wout_ref[...])
    for l in range(n_new):
        v_ref[:, l * dg:(l + 1) * dg] = v[l * nb:(l + 1) * nb, :]


def _sc_s_kernel(x_ref, st_ref, g_ref, win_ref, cw_ref, wout_ref, o_ref, nst_ref, *, n_new):
    nb = st_ref.shape[0]
    d = wout_ref.shape[0]
    x = x_ref[...]
    h = _rms(x, g_ref[...]).astype(BF16)
    z = _dot(h, win_ref[...])
    cx = z[:, d:2 * d] * z[:, 2 * d:]

    def ext(j):
        if j < SC_BUF:
            return _slab(st_ref, j, d)
        return cx[(j - SC_BUF) * nb:(j - SC_BUF + 1) * nb, :]

    convs = []
    for l in range(n_new):
        acc = ext(l) * cw_ref[0:1, :]
        for k in range(1, SC_WIDTH):
            acc = acc + ext(l + k) * cw_ref[k:k + 1, :]
        convs.append(acc)
    y = (z[:, :d] * jnp.concatenate(convs, axis=0)).astype(BF16)
    o_ref[...] = x + _dot(y, wout_ref[...])
    for j in range(SC_BUF):
        nst_ref[:, j * d:(j + 1) * d] = ext(j + n_new)


def _single_step(kern, operands, out_shape, scratch, name):
    return pl.pallas_call(
        kern,
        out_shape=out_shape,
        scratch_shapes=scratch,
        compiler_params=pltpu.CompilerParams(vmem_limit_bytes=VMEM_LIMIT),
        name=name,
    )(*operands)


def kernel(x_prompt, x_sample, state_pool, state_conv, state_shortconv, norm_mix, norm_ffn, norm_final, pool_w, pool_scale, conv_w_in, conv_b_in, conv_dw, conv_dw_b, conv_ln_g, conv_ln_b, conv_w_out, gm_w_in, gm_ln_g, gm_ln_b, gm_w_s, gm_b_s, gm_w_out, sc_w_in, sc_conv, sc_w_out, ffn_w_in, ffn_w_out):
    bp, sp, d = x_prompt.shape
    nb, n_new, _ = x_sample.shape
    depth = norm_mix.shape[0]
    assert depth == 4 and pool_w.shape[0] == 1, "one layer of each mixer type"
    assert sp % CHUNK == 0 and n_new < CHUNK and PAST_LEN % CHUNK == 0

    row = lambda a: a.reshape(1, -1)
    bf = lambda a: a.astype(BF16)
    rows_major = lambda a: jnp.transpose(a, (1, 0, 2))
    gfin = row(norm_final)
    gffn = norm_ffn.reshape(depth, 1, d)
    grp = d // N_GM_GROUPS
    m = nb * n_new
    tm_shape = jax.ShapeDtypeStruct((m, d), F32)

    def ffn(x, xs, i, w_in, w_out, casts):
        out = _ffn(x.reshape(bp * sp, d), xs, gffn, w_in, w_out, gfin, casts,
                   layer=i, final=(i == depth - 1))
        return out[0].reshape(bp, sp, d), out[1], out[2:]

    pool_consts = (row(norm_mix[0]), bf(pool_w[0]), row(pool_scale[0]))
    x, pool_p, f_in, f_out = _prompt_mixer(
        functools.partial(_pool_p_kernel, pos0=0), x_prompt, pool_consts, state_rows=POOL_BUF,
        scratch=lambda tm, dd: [(POOL_HALO + tm, dd), (2, POOL_HALO + tm, dd // len(POOL_WINDOWS))],
        name="pool_p", casts=[(ffn_w_in, 0), (ffn_w_out, 0)])
    xs, pool_s = _single_step(
        functools.partial(_pool_s_kernel, n_new=n_new, pos0=PAST_LEN),
        (rows_major(x_sample), rows_major(state_pool[0])) + pool_consts,
        [tm_shape, jax.ShapeDtypeStruct((POOL_BUF, nb, d), F32)],
        [pltpu.VMEM((m, d), F32)], "pool_s")
    x, xs, (cw_in, cw_out, f_in, f_out) = ffn(
        x, xs, 0, f_in, f_out, [(conv_w_in, 0), (conv_w_out, 0), (ffn_w_in, 1), (ffn_w_out, 1)])

    conv_consts = (row(norm_mix[1]), cw_in, row(conv_b_in[0]), conv_dw[0], row(conv_dw_b[0]),
                   row(conv_ln_g[0]), row(conv_ln_b[0]), cw_out)
    x, conv_p = _conv_p(x, conv_consts)
    xs, conv_s = _single_step(
        functools.partial(_conv_s_kernel, n_new=n_new),
        (xs, rows_major(state_conv[0])) + conv_consts,
        [tm_shape, jax.ShapeDtypeStruct((CONV_BUF, nb, d), F32)],
        [pltpu.VMEM((m, d), F32), pltpu.VMEM((m, d), F32)], "conv_s")
    x, xs, (gw_in, gw_out, f_in, f_out) = ffn(
        x, xs, 1, f_in, f_out, [(gm_w_in, 0), (gm_w_out, 0), (ffn_w_in, 2), (ffn_w_out, 2)])

    gm_head = (row(norm_mix[2]), gw_in, row(gm_ln_g[0]), row(gm_ln_b[0]))
    x, v_p = _prompt_mixer(
        _gm_p_kernel, x, gm_head + (gm_w_s[0], gm_b_s[0].T, gw_out), state_rows=CHUNK,
        scratch=lambda tm, dd: [], name="gm_p")
    wrow = jnp.repeat(jnp.transpose(gm_w_s[0][:, :n_new, :n_new], (1, 2, 0)).reshape(n_new * n_new, N_GM_GROUPS), grp, axis=1)
    brow = jnp.repeat(gm_b_s[0][:, :n_new].T, grp, axis=1)
    xs, v_s = _single_step(
        functools.partial(_gm_s_kernel, n_new=n_new, nb=nb),
        (xs,) + gm_head + (wrow, brow, gw_out),
        [tm_shape, jax.ShapeDtypeStruct((nb, n_new * d), F32)],
        [], "gm_s")
    x, xs, (sw_in, sw_out, f_in, f_out) = ffn(
        x, xs, 2, f_in, f_out, [(sc_w_in, 0), (sc_w_out, 0), (ffn_w_in, 3), (ffn_w_out, 3)])

    sc_consts = (row(norm_mix[3]), sw_in, sc_conv[0], sw_out)
    x, sc_p = _prompt_mixer(
        _sc_p_kernel, x, sc_consts, state_rows=SC_BUF,
        scratch=lambda tm, dd: [(SC_HALO + tm, dd)], name="sc_p")
    xs, sc_s = _single_step(
        functools.partial(_sc_s_kernel, n_new=n_new),
        (xs, state_shortconv[0].reshape(nb, SC_BUF * d)) + sc_consts,
        [tm_shape, jax.ShapeDtypeStruct((nb, SC_BUF * d), F32)],
        [], "sc_s")
    y_prompt, ys, _ = ffn(x, xs, 3, f_in, f_out, [])
    y_sample = jnp.transpose(ys.reshape(n_new, nb, d), (1, 0, 2))

    return (y_prompt, y_sample,
            pool_p[None], rows_major(pool_s)[None],
            conv_p[None], rows_major(conv_s)[None],
            v_p[None], v_s.reshape(1, nb, n_new, d),
            sc_p[None], sc_s.reshape(1, nb, SC_BUF, d))
```

```python
import functools

import jax
import jax.numpy as jnp
from jax import lax
from jax.experimental import pallas as pl
from jax.experimental.pallas import tpu as pltpu

F32 = jnp.float32
BF16 = jnp.bfloat16

EPS = 1e-6
POOL_WINDOWS = (2, 4, 8, 16)
POOL_BUF = max(POOL_WINDOWS) - 1
CONV_WIDTH = 31
CONV_BUF = CONV_WIDTH - 1
CHUNK = 128
N_GM_GROUPS = 4
SC_WIDTH = 3
SC_BUF = SC_WIDTH - 1
PAST_LEN = 16384

V7X_SUBLANES = 8
V7X_BF16_SUBLANES = 16
V7X_LANES = 128
V7X_MXU_DIM = 256
V7X_VMEM_BYTES = 64 * 1024 * 1024
VMEM_LIMIT = V7X_VMEM_BYTES - 4 * 1024 * 1024

TM = 1024
POOL_HALO = 32
CONV_HALO = 32
SC_HALO = 8
CONV_UNROLL = 16
CONV_GROUP = 8


def _params(n_grid):
    return pltpu.CompilerParams(
        dimension_semantics=("arbitrary",) * n_grid, vmem_limit_bytes=VMEM_LIMIT)


def _const_spec(shape):
    nd = len(shape)
    return pl.BlockSpec(shape, lambda *_: (0,) * nd, pipeline_mode=pl.Buffered(1))


def _rms(x, g):
    y = x * lax.rsqrt(jnp.mean(x * x, axis=-1, keepdims=True) + EPS)
    return y * g


def _ln(x, g, b):
    mu = jnp.mean(x, axis=-1, keepdims=True)
    xc = x - mu
    y = xc * lax.rsqrt(jnp.mean(xc * xc, axis=-1, keepdims=True) + EPS)
    return y * g + b


def _sigmoid(x):
    return 0.5 * (1.0 + jnp.tanh(0.5 * x))


def _silu(x):
    return x * _sigmoid(x)


def _gelu_tanh(x):
    c = 0.7978845608028654
    return x * (0.5 * (1.0 + jnp.tanh(c * (x + 0.044715 * (x * x * x)))))


def _dot(a, b):
    return jnp.dot(a, b, preferred_element_type=F32)


def _ffn_chunks(d_ff, n_chunks):
    tiles = d_ff // V7X_MXU_DIM
    assert tiles * V7X_MXU_DIM == d_ff
    edges = [V7X_MXU_DIM * ((tiles * i + n_chunks - 1) // n_chunks) for i in range(n_chunks + 1)]
    return list(zip(edges[:-1], edges[1:]))


def _ffn_kernel(*refs, d_ff, chunks, final, n_cast, n_prompt_steps):
    xp_ref, xs_ref, g_ref, win_ref, wout_ref, gf_ref = refs[:6]
    cast_src = refs[6:6 + n_cast]
    op_ref, os_ref = refs[6 + n_cast:8 + n_cast]
    cast_dst = refs[8 + n_cast:]
    i = pl.program_id(0)

    def rows(x_ref, o_ref):
        x = x_ref[...]
        h = _rms(x, g_ref[...]).astype(BF16)
        acc = x
        for c0, c1 in chunks:
            gate = _dot(h, win_ref[:, c0:c1])
            up = _dot(h, win_ref[:, d_ff + c0:d_ff + c1])
            act = (_silu(gate) * up).astype(BF16)
            acc = acc + _dot(act, wout_ref[c0:c1, :])
        if final:
            acc = _rms(acc, gf_ref[...])
        o_ref[...] = acc

    @pl.when(i < n_prompt_steps)
    def _():
        rows(xp_ref, op_ref)
        for src, dst in zip(cast_src, cast_dst):
            dst[...] = src[...].astype(BF16)

    @pl.when(i == n_prompt_steps)
    def _():
        rows(xs_ref, os_ref)


def _layer_spec(shape, layer):
    nd = len(shape)
    return pl.BlockSpec((None,) + tuple(shape[1:]), lambda *_: (layer,) + (0,) * (nd - 1),
                        pipeline_mode=pl.Buffered(1))


def _cast_specs(casts, n_blocks, block_of):
    cast_in, cast_out, cast_shape = [], [], []
    for arr, idx in casts:
        _, r, c = arr.shape
        assert r % (n_blocks * V7X_BF16_SUBLANES) == 0
        cast_in.append(pl.BlockSpec((None, r // n_blocks, c),
                                    lambda *g, idx=idx: (idx, block_of(*g), 0)))
        cast_out.append(pl.BlockSpec((r // n_blocks, c), lambda *g: (block_of(*g), 0)))
        cast_shape.append(jax.ShapeDtypeStruct((r, c), BF16))
    return cast_in, cast_out, cast_shape


def _ffn(xp, xs, g, w_in, w_out, g_final, casts, *, layer, final):
    mp, d = xp.shape
    ms = xs.shape[0]
    d_ff = w_out.shape[0]
    assert mp % TM == 0
    n_steps = mp // TM
    last = n_steps - 1
    kern = functools.partial(_ffn_kernel, d_ff=d_ff, chunks=_ffn_chunks(d_ff, 2), final=final,
                             n_cast=len(casts), n_prompt_steps=n_steps)
    cast_in, cast_out, cast_shape = _cast_specs(casts, n_steps, lambda i: jnp.minimum(i, last))
    return pl.pallas_call(
        kern,
        grid=(n_steps + 1,),
        in_specs=[
            pl.BlockSpec((TM, d), lambda i: (jnp.minimum(i, last), 0)),
            _const_spec((ms, d)),
            _layer_spec(g.shape, layer),
            _const_spec(w_in.shape),
            _const_spec(w_out.shape),
            _const_spec((1, d)),
        ] + cast_in,
        out_specs=[
            pl.BlockSpec((TM, d), lambda i: (jnp.minimum(i, last), 0)),
            pl.BlockSpec((ms, d), lambda i: (0, 0)),
        ] + cast_out,
        out_shape=[jax.ShapeDtypeStruct((mp, d), F32), jax.ShapeDtypeStruct((ms, d), F32)] + cast_shape,
        compiler_params=_params(1),
        name="ffn_final" if final else "ffn",
    )(xp, xs, g, w_in, w_out, g_final, *[arr for arr, _ in casts])


def _carry_halo(ext_ref, halo, tm):
    ext_ref[0:halo, :] = ext_ref[tm:tm + halo, :]


def _pool_p_kernel(*refs, tm, pos0, n_cast):
    x_ref, g_ref, w_ref, sc_ref = refs[:4]
    cast_src = refs[4:4 + n_cast]
    o_ref, st_ref = refs[4 + n_cast:6 + n_cast]
    cast_dst = refs[6 + n_cast:6 + 2 * n_cast]
    ext_ref, tmp_ref = refs[6 + 2 * n_cast:]
    s = pl.program_id(1)
    grp = w_ref.shape[1]
    halo = POOL_HALO
    end = halo + tm

    @pl.when(s == 0)
    def _():
        ext_ref[0:halo, :] = jnp.zeros((halo, ext_ref.shape[1]), F32)

    x = x_ref[...]
    h = _rms(x, g_ref[...])
    ext_ref[halo:end, :] = h
    pos = pos0 + s * tm + lax.broadcasted_iota(jnp.int32, (tm, 1), 0)
    ys = []
    for gi, win in enumerate(POOL_WINDOWS):
        lanes = slice(gi * grp, (gi + 1) * grp)
        assert win == 1 << (win.bit_length() - 1) and V7X_SUBLANES * (win.bit_length() - 1) <= halo
        read = lambda r0, r1: ext_ref[r0:r1, lanes]
        shift, level = 1, 1
        while 2 * shift < win:
            r0 = V7X_SUBLANES * level
            slot = level % 2
            tmp_ref[slot, r0:end, :] = read(r0, end) + read(r0 - shift, end - shift)
            read = lambda r0, r1, slot=slot: tmp_ref[slot, r0:r1, :]
            shift, level = 2 * shift, level + 1
        acc = read(halo, end) + read(halo - shift, end - shift)
        inv_cnt = 1.0 / jnp.minimum(win, pos + 1).astype(F32)
        d = acc * inv_cnt - h[:, lanes]
        ys.append(_dot(d.astype(BF16), w_ref[gi]))
    y = jnp.concatenate(ys, axis=1)
    o_ref[...] = x + y * sc_ref[...]
    for src, dst in zip(cast_src, cast_dst):
        dst[...] = src[...].astype(BF16)

    @pl.when(s == pl.num_programs(1) - 1)
    def _():
        st_ref[...] = ext_ref[end - POOL_BUF:end, :]

    _carry_halo(ext_ref, halo, tm)


def _conv_p_kernel(x_ref, g_ref, win_ref, bin_ref, dw_ref, dwb_ref, lng_ref, lnb_ref, wout_ref,
                   o_ref, st_ref, glu_ref, ext_ref, c_ref, wb_ref, *, nb, ts):
    s = pl.program_id(0)
    d = x_ref.shape[2]
    n_lt, _, lane = glu_ref.shape
    pitch = glu_ref.shape[1] // nb

    @pl.when(s == 0)
    def _():
        ext_ref[:, 0:CONV_HALO * nb, :] = jnp.zeros((n_lt, CONV_HALO * nb, lane), F32)
        for j in range(n_lt):
            for k in range(CONV_WIDTH):
                wb_ref[j, k * nb:(k + 1) * nb, :] = jnp.broadcast_to(
                    dw_ref[k:k + 1, j * lane:(j + 1) * lane], (nb, lane))

    x = x_ref[...].reshape(nb * ts, d)
    h = _rms(x, g_ref[...]).astype(BF16)
    z = _dot(h, win_ref[...]) + bin_ref[...]
    dc = n_lt * lane
    glu = z[:, :dc] * _sigmoid(z[:, dc:])
    base = CONV_HALO - CONV_BUF
    for j in range(n_lt):
        lanes = slice(j * lane, (j + 1) * lane)
        for b in range(nb):
            glu_ref[j, b * pitch:b * pitch + ts, :] = glu[b * ts:(b + 1) * ts, lanes]
        for t in range(ts):
            ext_ref[j, (CONV_HALO + t) * nb:(CONV_HALO + t + 1) * nb, :] = (
                glu_ref[j, pl.ds(t, nb, stride=pitch), :])

        bias = jnp.broadcast_to(dwb_ref[:, lanes], (nb, lane))
        for t0 in range(0, ts, CONV_UNROLL):
            acc = [bias] * CONV_UNROLL
            for k0 in range(0, CONV_WIDTH, CONV_GROUP):
                n_taps = min(CONV_GROUP, CONV_WIDTH - k0)
                r0 = t0 + base + k0
                rows = [ext_ref[j, (r0 + m) * nb:(r0 + m + 1) * nb, :]
                        for m in range(CONV_UNROLL + n_taps - 1)]
                for kk in range(n_taps):
                    w = wb_ref[j, (k0 + kk) * nb:(k0 + kk + 1) * nb, :]
                    acc = [a + rows[u + kk] * w for u, a in enumerate(acc)]
            for u in range(CONV_UNROLL):
                c_ref[j, (t0 + u) * nb:(t0 + u + 1) * nb, :] = acc[u]

    c = jnp.concatenate(
        [jnp.concatenate([c_ref[j, pl.ds(b, ts, stride=nb), :] for b in range(nb)], axis=0)
         for j in range(n_lt)], axis=1)
    c = _silu(_ln(c, lng_ref[...], lnb_ref[...]))
    o_ref[...] = (x + _dot(c.astype(BF16), wout_ref[...])).reshape(nb, ts, d)

    @pl.when(s == pl.num_programs(0) - 1)
    def _():
        for b in range(nb):
            for j in range(n_lt):
                st_ref[b, :, j * lane:(j + 1) * lane] = (
                    glu_ref[j, b * pitch + ts - CONV_BUF:b * pitch + ts, :])

    ext_ref[:, 0:CONV_HALO * nb, :] = ext_ref[:, ts * nb:(ts + CONV_HALO) * nb, :]


def _conv_p(x, consts):
    nb, s, d = x.shape
    dc = consts[-1].shape[0]
    assert nb == V7X_SUBLANES, "the time-major copy puts one sequence on each sublane"
    ts = TM // nb
    assert s % ts == 0 and ts >= CONV_HALO
    return pl.pallas_call(
        functools.partial(_conv_p_kernel, nb=nb, ts=ts),
        grid=(s // ts,),
        in_specs=[pl.BlockSpec((nb, ts, d), lambda j: (0, j, 0))] + [_const_spec(c.shape) for c in consts],
        out_specs=[
            pl.BlockSpec((nb, ts, d), lambda j: (0, j, 0)),
            pl.BlockSpec((nb, CONV_BUF, dc), lambda j: (0, 0, 0)),
        ],
        out_shape=[
            jax.ShapeDtypeStruct((nb, s, d), F32),
            jax.ShapeDtypeStruct((nb, CONV_BUF, dc), F32),
        ],
        scratch_shapes=[
            pltpu.VMEM((dc // V7X_LANES, nb * (ts + V7X_SUBLANES), V7X_LANES), F32),
            pltpu.VMEM((dc // V7X_LANES, (CONV_HALO + ts) * nb, V7X_LANES), F32),
            pltpu.VMEM((dc // V7X_LANES, ts * nb, V7X_LANES), F32),
            pltpu.VMEM((dc // V7X_LANES, CONV_WIDTH * nb, V7X_LANES), F32),
        ],
        compiler_params=_params(1),
        name="conv_p",
    )(x, *consts)


def _gm_p_kernel(x_ref, g_ref, win_ref, lng_ref, lnb_ref, ws_ref, bs_ref, wout_ref,
                 o_ref, v_ref, *, tm):
    s = pl.program_id(1)
    dg = wout_ref.shape[0]
    grp = dg // N_GM_GROUPS
    row = lax.broadcasted_iota(jnp.int32, (CHUNK, CHUNK), 0)
    col = lax.broadcasted_iota(jnp.int32, (CHUNK, CHUNK), 1)
    causal = col <= row
    x = x_ref[...]
    h = _rms(x, g_ref[...]).astype(BF16)
    z = _gelu_tanh(_dot(h, win_ref[...]))
    u = z[:, :dg]
    v = _ln(z[:, dg:], lng_ref[...], lnb_ref[...])
    vb = v.astype(BF16)
    mixed_cols = []
    for gi in range(N_GM_GROUPS):
        wsg = jnp.where(causal, ws_ref[gi], 0.0).astype(BF16)
        bias = bs_ref[:, gi:gi + 1]
        parts = [_dot(wsg, vb[c0:c0 + CHUNK, gi * grp:(gi + 1) * grp]) + bias
                 for c0 in range(0, tm, CHUNK)]
        mixed_cols.append(jnp.concatenate(parts, axis=0))
    mixed = jnp.concatenate(mixed_cols, axis=1)
    y = (u * mixed).astype(BF16)
    o_ref[...] = x + _dot(y, wout_ref[...])

    @pl.when(s == pl.num_programs(1) - 1)
    def _():
        v_ref[...] = v[tm - CHUNK:, :]


def _sc_p_kernel(x_ref, g_ref, win_ref, cw_ref, wout_ref, o_ref, st_ref, ext_ref, *, tm):
    s = pl.program_id(1)
    d = wout_ref.shape[0]

    @pl.when(s == 0)
    def _():
        ext_ref[0:SC_HALO, :] = jnp.zeros((SC_HALO, d), F32)

    x = x_ref[...]
    h = _rms(x, g_ref[...]).astype(BF16)
    z = _dot(h, win_ref[...])
    cx = z[:, d:2 * d] * z[:, 2 * d:]
    ext_ref[SC_HALO:SC_HALO + tm, :] = cx
    conv = cx * cw_ref[SC_WIDTH - 1:SC_WIDTH, :]
    for k in range(SC_WIDTH - 1):
        off = SC_HALO - (SC_WIDTH - 1) + k
        conv = conv + ext_ref[off:off + tm, :] * cw_ref[k:k + 1, :]
    y = (z[:, :d] * conv).astype(BF16)
    o_ref[...] = x + _dot(y, wout_ref[...])

    @pl.when(s == pl.num_programs(1) - 1)
    def _():
        st_ref[...] = ext_ref[SC_HALO + tm - SC_BUF:SC_HALO + tm, :]

    _carry_halo(ext_ref, SC_HALO, tm)


def _prompt_mixer(kern, x, consts, *, state_rows, scratch, name, casts=()):
    b, s, d = x.shape
    tm = min(TM, s)
    n_j = s // tm
    cast_in, cast_out, cast_shape = _cast_specs(casts, b * n_j, lambda i, j: i * n_j + j)
    if casts:
        kern = functools.partial(kern, n_cast=len(casts))
    return pl.pallas_call(
        functools.partial(kern, tm=tm),
        grid=(b, n_j),
        in_specs=[pl.BlockSpec((None, tm, d), lambda i, j: (i, j, 0))]
        + [_const_spec(c.shape) for c in consts] + cast_in,
        out_specs=[
            pl.BlockSpec((None, tm, d), lambda i, j: (i, j, 0)),
            pl.BlockSpec((None, state_rows, d), lambda i, j: (i, 0, 0)),
        ] + cast_out,
        out_shape=[
            jax.ShapeDtypeStruct((b, s, d), F32),
            jax.ShapeDtypeStruct((b, state_rows, d), F32),
        ] + cast_shape,
        scratch_shapes=[pltpu.VMEM(shape, F32) for shape in scratch(tm, d)],
        compiler_params=_params(2),
        name=name,
    )(x, *consts, *[arr for arr, _ in casts])


def _slab(ref, j, d):
    return ref[:, j * d:(j + 1) * d]


def _pool_s_kernel(x_ref, st_ref, g_ref, w_ref, sc_ref, o_ref, nst_ref, d_ref, *, n_new, pos0):
    nb = x_ref.shape[1]
    grp = w_ref.shape[1]
    xs = [x_ref[l] for l in range(n_new)]
    hs = [_rms(x, g_ref[...]) for x in xs]

    def ext(j, c0):
        if j < POOL_BUF:
            return st_ref[j, :, c0:c0 + grp]
        return hs[j - POOL_BUF][:, c0:c0 + grp]

    for l in range(n_new):
        for gi, win in enumerate(POOL_WINDOWS):
            c0 = gi * grp
            acc = ext(POOL_BUF + l, c0)
            for k in range(1, win):
                acc = acc + ext(POOL_BUF + l - k, c0)
            cnt = float(min(win, pos0 + l + 1))
            d_ref[l * nb:(l + 1) * nb, c0:c0 + grp] = acc / cnt - hs[l][:, c0:c0 + grp]
    db = d_ref[...].astype(BF16)
    y = jnp.concatenate(
        [_dot(db[:, gi * grp:(gi + 1) * grp], w_ref[gi]) for gi in range(len(POOL_WINDOWS))], axis=1)
    y = y * sc_ref[...]
    for l in range(n_new):
        o_ref[l * nb:(l + 1) * nb, :] = xs[l] + y[l * nb:(l + 1) * nb, :]
    for j in range(POOL_BUF):
        src = j + n_new
        nst_ref[j] = st_ref[src] if src < POOL_BUF else hs[src - POOL_BUF]


def _conv_s_kernel(x_ref, st_hbm, g_ref, win_ref, bin_ref, dw_ref, dwb_ref, lng_ref, lnb_ref,
                   wout_ref, o_ref, nst_hbm, st_vmem, g_scr, c_scr, sem, *, n_new):
    nb = st_vmem.shape[1]
    dc = wout_ref.shape[0]
    keep = CONV_BUF - n_new
    load_old = pltpu.make_async_copy(st_hbm, st_vmem, sem.at[0])
    shift_old = pltpu.make_async_copy(
        st_hbm.at[pl.ds(n_new, keep)], nst_hbm.at[pl.ds(0, keep)], sem.at[1])
    store_new = pltpu.make_async_copy(g_scr, nst_hbm.at[pl.ds(keep, n_new)], sem.at[2])
    load_old.start()
    shift_old.start()

    x = x_ref[...]
    h = _rms(x, g_ref[...]).astype(BF16)
    z = _dot(h, win_ref[...]) + bin_ref[...]
    glu = z[:, :dc] * _sigmoid(z[:, dc:])
    for l in range(n_new):
        g_scr[l] = glu[l * nb:(l + 1) * nb, :]
    store_new.start()
    load_old.wait()

    def ext(j):
        return st_vmem[j] if j < CONV_BUF else g_scr[j - CONV_BUF]

    for l in range(n_new):
        acc = jnp.zeros((nb, dc), F32) + dwb_ref[...]
        for k in range(CONV_WIDTH):
            acc = acc + ext(l + k) * dw_ref[k:k + 1, :]
        c_scr[l * nb:(l + 1) * nb, :] = acc
    c = _silu(_ln(c_scr[...], lng_ref[...], lnb_ref[...]))
    o_ref[...] = x + _dot(c.astype(BF16), wout_ref[...])
    store_new.wait()
    shift_old.wait()


def _conv_s(xs, state, consts, *, n_new):
    m, d = xs.shape
    rows, nb, dc = state.shape
    assert rows == CONV_BUF and m == n_new * nb and 0 < n_new < CONV_BUF
    vmem = pl.BlockSpec(memory_space=pltpu.VMEM)
    hbm = pl.BlockSpec(memory_space=pl.ANY)
    return pl.pallas_call(
        functools.partial(_conv_s_kernel, n_new=n_new),
        in_specs=[vmem, hbm] + [vmem] * len(consts),
        out_specs=[vmem, hbm],
        out_shape=[jax.ShapeDtypeStruct((m, d), F32), jax.ShapeDtypeStruct(state.shape, F32)],
        scratch_shapes=[
            pltpu.VMEM(state.shape, F32),
            pltpu.VMEM((n_new, nb, dc), F32),
            pltpu.VMEM((m, dc), F32),
            pltpu.SemaphoreType.DMA((3,)),
        ],
        compiler_params=pltpu.CompilerParams(vmem_limit_bytes=VMEM_LIMIT),
        name="conv_s",
    )(xs, state, *consts)


def _gm_s_kernel(x_ref, g_ref, win_ref, lng_ref, lnb_ref, wrow_ref, brow_ref, wout_ref,
                 o_ref, v_ref, *, n_new, nb):
    dg = wout_ref.shape[0]
    x = x_ref[...]
    h = _rms(x, g_ref[...]).astype(BF16)
    z = _gelu_tanh(_dot(h, win_ref[...]))
    u = z[:, :dg]
    v = _ln(z[:, dg:], lng_ref[...], lnb_ref[...])
    mixed = []
    for t in range(n_new):
        acc = jnp.zeros((nb, dg), F32) + brow_ref[t:t + 1, :]
        for s in range(t + 1):
            acc = acc + wrow_ref[t * n_new + s:t * n_new + s + 1, :] * v[s * nb:(s + 1) * nb, :]
        mixed.append(acc)
    y = (u * jnp.concatenate(mixed, axis=0)).astype(BF16)
    o_ref[...] = x + _dot(y, wout_ref[...])
    for l in range(n_new):
        v_ref[:, l * dg:(l + 1) * dg] = v[l * nb:(l + 1) * nb, :]


def _sc_s_kernel(x_ref, st_ref, g_ref, win_ref, cw_ref, wout_ref, o_ref, nst_ref, *, n_new):
    nb = st_ref.shape[0]
    d = wout_ref.shape[0]
    x = x_ref[...]
    h = _rms(x, g_ref[...]).astype(BF16)
    z = _dot(h, win_ref[...])
    cx = z[:, d:2 * d] * z[:, 2 * d:]

    def ext(j):
        if j < SC_BUF:
            return _slab(st_ref, j, d)
        return cx[(j - SC_BUF) * nb:(j - SC_BUF + 1) * nb, :]

    convs = []
    for l in range(n_new):
        acc = ext(l) * cw_ref[0:1, :]
        for k in range(1, SC_WIDTH):
            acc = acc + ext(l + k) * cw_ref[k:k + 1, :]
        convs.append(acc)
    y = (z[:, :d] * jnp.concatenate(convs, axis=0)).astype(BF16)
    o_ref[...] = x + _dot(y, wout_ref[...])
    for j in range(SC_BUF):
        nst_ref[:, j * d:(j + 1) * d] = ext(j + n_new)


def _single_step(kern, operands, out_shape, scratch, name):
    return pl.pallas_call(
        kern,
        out_shape=out_shape,
        scratch_shapes=scratch,
        compiler_params=pltpu.CompilerParams(vmem_limit_bytes=VMEM_LIMIT),
        name=name,
    )(*operands)


def kernel(x_prompt, x_sample, state_pool, state_conv, state_shortconv, norm_mix, norm_ffn, norm_final, pool_w, pool_scale, conv_w_in, conv_b_in, conv_dw, conv_dw_b, conv_ln_g, conv_ln_b, conv_w_out, gm_w_in, gm_ln_g, gm_ln_b, gm_w_s, gm_b_s, gm_w_out, sc_w_in, sc_conv, sc_w_out, ffn_w_in, ffn_w_out):
    bp, sp, d = x_prompt.shape
    nb, n_new, _ = x_sample.shape
    depth = norm_mix.shape[0]
    assert depth == 4 and pool_w.shape[0] == 1, "one layer of each mixer type"
    assert sp % CHUNK == 0 and n_new < CHUNK and PAST_LEN % CHUNK == 0

    row = lambda a: a.reshape(1, -1)
    bf = lambda a: a.astype(BF16)
    rows_major = lambda a: jnp.transpose(a, (1, 0, 2))
    gfin = row(norm_final)
    gffn = norm_ffn.reshape(depth, 1, d)
    grp = d // N_GM_GROUPS
    m = nb * n_new
    tm_shape = jax.ShapeDtypeStruct((m, d), F32)

    def ffn(x, xs, i, w_in, w_out, casts):
        out = _ffn(x.reshape(bp * sp, d), xs, gffn, w_in, w_out, gfin, casts,
                   layer=i, final=(i == depth - 1))
        return out[0].reshape(bp, sp, d), out[1], out[2:]

    pool_consts = (row(norm_mix[0]), bf(pool_w[0]), row(pool_scale[0]))
    x, pool_p, f_in, f_out = _prompt_mixer(
        functools.partial(_pool_p_kernel, pos0=0), x_prompt, pool_consts, state_rows=POOL_BUF,
        scratch=lambda tm, dd: [(POOL_HALO + tm, dd), (2, POOL_HALO + tm, dd // len(POOL_WINDOWS))],
        name="pool_p", casts=[(ffn_w_in, 0), (ffn_w_out, 0)])
    xs, pool_s = _single_step(
        functools.partial(_pool_s_kernel, n_new=n_new, pos0=PAST_LEN),
        (rows_major(x_sample), rows_major(state_pool[0])) + pool_consts,
        [tm_shape, jax.ShapeDtypeStruct((POOL_BUF, nb, d), F32)],
        [pltpu.VMEM((m, d), F32)], "pool_s")
    x, xs, (cw_in, cw_out, f_in, f_out) = ffn(
        x, xs, 0, f_in, f_out, [(conv_w_in, 0), (conv_w_out, 0), (ffn_w_in, 1), (ffn_w_out, 1)])

    conv_consts = (row(norm_mix[1]), cw_in, row(conv_b_in[0]), conv_dw[0], row(conv_dw_b[0]),
                   row(conv_ln_g[0]), row(conv_ln_b[0]), cw_out)
    x, conv_p = _conv_p(x, conv_consts)
    xs, conv_s = _conv_s(xs, rows_major(state_conv[0]), conv_consts, n_new=n_new)
    x, xs, (gw_in, gw_out, f_in, f_out) = ffn(
        x, xs, 1, f_in, f_out, [(gm_w_in, 0), (gm_w_out, 0), (ffn_w_in, 2), (ffn_w_out, 2)])

    gm_head = (row(norm_mix[2]), gw_in, row(gm_ln_g[0]), row(gm_ln_b[0]))
    x, v_p = _prompt_mixer(
        _gm_p_kernel, x, gm_head + (gm_w_s[0], gm_b_s[0].T, gw_out), state_rows=CHUNK,
        scratch=lambda tm, dd: [], name="gm_p")
    wrow = jnp.repeat(jnp.transpose(gm_w_s[0][:, :n_new, :n_new], (1, 2, 0)).reshape(n_new * n_new, N_GM_GROUPS), grp, axis=1)
    brow = jnp.repeat(gm_b_s[0][:, :n_new].T, grp, axis=1)
    xs, v_s = _single_step(
        functools.partial(_gm_s_kernel, n_new=n_new, nb=nb),
        (xs,) + gm_head + (wrow, brow, gw_out),
        [tm_shape, jax.ShapeDtypeStruct((nb, n_new * d), F32)],
        [], "gm_s")
    x, xs, (sw_in, sw_out, f_in, f_out) = ffn(
        x, xs, 2, f_in, f_out, [(sc_w_in, 0), (sc_w_out, 0), (ffn_w_in, 3), (ffn_w_out, 3)])

    sc_consts = (row(norm_mix[3]), sw_in, sc_conv[0], sw_out)
    x, sc_p = _prompt_mixer(
        _sc_p_kernel, x, sc_consts, state_rows=SC_BUF,
        scratch=lambda tm, dd: [(SC_HALO + tm, dd)], name="sc_p")
    xs, sc_s = _single_step(
        functools.partial(_sc_s_kernel, n_new=n_new),
        (xs, state_shortconv[0].reshape(nb, SC_BUF * d)) + sc_consts,
        [tm_shape, jax.ShapeDtypeStruct((nb, SC_BUF * d), F32)],
        [], "sc_s")
    y_prompt, ys, _ = ffn(x, xs, 3, f_in, f_out, [])
    y_sample = jnp.transpose(ys.reshape(n_new, nb, d), (1, 0, 2))

    return (y_prompt, y_sample,
            pool_p[None], rows_major(pool_s)[None],
            conv_p[None], rows_major(conv_s)[None],
            v_p[None], v_s.reshape(1, nb, n_new, d),
            sc_p[None], sc_s.reshape(1, nb, SC_BUF, d))
```

```python
import functools

import jax
import jax.numpy as jnp
from jax import lax
from jax.experimental import pallas as pl
from jax.experimental.pallas import tpu as pltpu

F32 = jnp.float32
BF16 = jnp.bfloat16

EPS = 1e-6
POOL_WINDOWS = (2, 4, 8, 16)
POOL_BUF = max(POOL_WINDOWS) - 1
CONV_WIDTH = 31
CONV_BUF = CONV_WIDTH - 1
CHUNK = 128
N_GM_GROUPS = 4
SC_WIDTH = 3
SC_BUF = SC_WIDTH - 1
PAST_LEN = 16384

V7X_SUBLANES = 8
V7X_BF16_SUBLANES = 16
V7X_LANES = 128
V7X_MXU_DIM = 256
V7X_VMEM_BYTES = 64 * 1024 * 1024
VMEM_LIMIT = V7X_VMEM_BYTES - 4 * 1024 * 1024

TM = 1024
POOL_HALO = 32
CONV_HALO = 32
SC_HALO = 8
CONV_UNROLL = 16
CONV_GROUP = 8


def _params(n_grid):
    return pltpu.CompilerParams(
        dimension_semantics=("arbitrary",) * n_grid, vmem_limit_bytes=VMEM_LIMIT)


def _const_spec(shape):
    nd = len(shape)
    return pl.BlockSpec(shape, lambda *_: (0,) * nd, pipeline_mode=pl.Buffered(1))


def _rms(x, g):
    y = x * lax.rsqrt(jnp.mean(x * x, axis=-1, keepdims=True) + EPS)
    return y * g


def _ln(x, g, b):
    mu = jnp.mean(x, axis=-1, keepdims=True)
    xc = x - mu
    y = xc * lax.rsqrt(jnp.mean(xc * xc, axis=-1, keepdims=True) + EPS)
    return y * g + b


def _sigmoid(x):
    return 0.5 * (1.0 + jnp.tanh(0.5 * x))


def _silu(x):
    return x * _sigmoid(x)


def _gelu_tanh(x):
    c = 0.7978845608028654
    return x * (0.5 * (1.0 + jnp.tanh(c * (x + 0.044715 * (x * x * x)))))


def _dot(a, b):
    return jnp.dot(a, b, preferred_element_type=F32)


def _ffn_chunks(d_ff, n_chunks):
    tiles = d_ff // V7X_MXU_DIM
    assert tiles * V7X_MXU_DIM == d_ff
    edges = [V7X_MXU_DIM * ((tiles * i + n_chunks - 1) // n_chunks) for i in range(n_chunks + 1)]
    return list(zip(edges[:-1], edges[1:]))


def _ffn_kernel(*refs, d_ff, chunks, final, n_cast, n_prompt_steps):
    xp_ref, xs_ref, g_ref, win_ref, wout_ref, gf_ref = refs[:6]
    cast_src = refs[6:6 + n_cast]
    op_ref, os_ref = refs[6 + n_cast:8 + n_cast]
    cast_dst = refs[8 + n_cast:]
    i = pl.program_id(0)

    def rows(x_ref, o_ref):
        x = x_ref[...]
        h = _rms(x, g_ref[...]).astype(BF16)
        acc = x
        for c0, c1 in chunks:
            gate = _dot(h, win_ref[:, c0:c1])
            up = _dot(h, win_ref[:, d_ff + c0:d_ff + c1])
            act = (_silu(gate) * up).astype(BF16)
            acc = acc + _dot(act, wout_ref[c0:c1, :])
        if final:
            acc = _rms(acc, gf_ref[...])
        o_ref[...] = acc

    @pl.when(i < n_prompt_steps)
    def _():
        rows(xp_ref, op_ref)
        for src, dst in zip(cast_src, cast_dst):
            dst[...] = src[...].astype(BF16)

    @pl.when(i == n_prompt_steps)
    def _():
        rows(xs_ref, os_ref)


def _layer_spec(shape, layer):
    nd = len(shape)
    return pl.BlockSpec((None,) + tuple(shape[1:]), lambda *_: (layer,) + (0,) * (nd - 1),
                        pipeline_mode=pl.Buffered(1))


def _cast_specs(casts, n_blocks, block_of):
    cast_in, cast_out, cast_shape = [], [], []
    for arr, idx in casts:
        _, r, c = arr.shape
        assert r % (n_blocks * V7X_BF16_SUBLANES) == 0
        cast_in.append(pl.BlockSpec((None, r // n_blocks, c),
                                    lambda *g, idx=idx: (idx, block_of(*g), 0)))
        cast_out.append(pl.BlockSpec((r // n_blocks, c), lambda *g: (block_of(*g), 0)))
        cast_shape.append(jax.ShapeDtypeStruct((r, c), BF16))
    return cast_in, cast_out, cast_shape


def _ffn(xp, xs, g, w_in, w_out, g_final, casts, *, layer, final):
    mp, d = xp.shape
    ms = xs.shape[0]
    d_ff = w_out.shape[0]
    assert mp % TM == 0
    n_steps = mp // TM
    last = n_steps - 1
    kern = functools.partial(_ffn_kernel, d_ff=d_ff, chunks=_ffn_chunks(d_ff, 2), final=final,
                             n_cast=len(casts), n_prompt_steps=n_steps)
    cast_in, cast_out, cast_shape = _cast_specs(casts, n_steps, lambda i: jnp.minimum(i, last))
    return pl.pallas_call(
        kern,
        grid=(n_steps + 1,),
        in_specs=[
            pl.BlockSpec((TM, d), lambda i: (jnp.minimum(i, last), 0)),
            _const_spec((ms, d)),
            _layer_spec(g.shape, layer),
            _const_spec(w_in.shape),
            _const_spec(w_out.shape),
            _const_spec((1, d)),
        ] + cast_in,
        out_specs=[
            pl.BlockSpec((TM, d), lambda i: (jnp.minimum(i, last), 0)),
            pl.BlockSpec((ms, d), lambda i: (0, 0)),
        ] + cast_out,
        out_shape=[jax.ShapeDtypeStruct((mp, d), F32), jax.ShapeDtypeStruct((ms, d), F32)] + cast_shape,
        compiler_params=_params(1),
        name="ffn_final" if final else "ffn",
    )(xp, xs, g, w_in, w_out, g_final, *[arr for arr, _ in casts])


def _carry_halo(ext_ref, halo, tm):
    ext_ref[0:halo, :] = ext_ref[tm:tm + halo, :]


def _pool_p_kernel(*refs, tm, pos0, n_cast):
    x_ref, g_ref, w_ref, sc_ref = refs[:4]
    cast_src = refs[4:4 + n_cast]
    o_ref, st_ref = refs[4 + n_cast:6 + n_cast]
    cast_dst = refs[6 + n_cast:6 + 2 * n_cast]
    ext_ref, tmp_ref = refs[6 + 2 * n_cast:]
    s = pl.program_id(1)
    grp = w_ref.shape[1]
    halo = POOL_HALO
    end = halo + tm

    @pl.when(s == 0)
    def _():
        ext_ref[0:halo, :] = jnp.zeros((halo, ext_ref.shape[1]), F32)

    x = x_ref[...]
    h = _rms(x, g_ref[...])
    ext_ref[halo:end, :] = h
    pos = pos0 + s * tm + lax.broadcasted_iota(jnp.int32, (tm, 1), 0)
    ys = []
    for gi, win in enumerate(POOL_WINDOWS):
        lanes = slice(gi * grp, (gi + 1) * grp)
        assert win == 1 << (win.bit_length() - 1) and V7X_SUBLANES * (win.bit_length() - 1) <= halo
        read = lambda r0, r1: ext_ref[r0:r1, lanes]
        shift, level = 1, 1
        while 2 * shift < win:
            r0 = V7X_SUBLANES * level
            slot = level % 2
            tmp_ref[slot, r0:end, :] = read(r0, end) + read(r0 - shift, end - shift)
            read = lambda r0, r1, slot=slot: tmp_ref[slot, r0:r1, :]
            shift, level = 2 * shift, level + 1
        acc = read(halo, end) + read(halo - shift, end - shift)
        inv_cnt = 1.0 / jnp.minimum(win, pos + 1).astype(F32)
        d = acc * inv_cnt - h[:, lanes]
        ys.append(_dot(d.astype(BF16), w_ref[gi]))
    y = jnp.concatenate(ys, axis=1)
    o_ref[...] = x + y * sc_ref[...]
    for src, dst in zip(cast_src, cast_dst):
        dst[...] = src[...].astype(BF16)

    @pl.when(s == pl.num_programs(1) - 1)
    def _():
        st_ref[...] = ext_ref[end - POOL_BUF:end, :]

    _carry_halo(ext_ref, halo, tm)


def _conv_p_kernel(x_ref, g_ref, win_ref, bin_ref, dw_ref, dwb_ref, lng_ref, lnb_ref, wout_ref,
                   o_ref, st_ref, glu_ref, ext_ref, c_ref, wb_ref, *, nb, ts):
    s = pl.program_id(0)
    d = x_ref.shape[2]
    n_lt, _, lane = glu_ref.shape
    pitch = glu_ref.shape[1] // nb

    @pl.when(s == 0)
    def _():
        ext_ref[:, 0:CONV_HALO * nb, :] = jnp.zeros((n_lt, CONV_HALO * nb, lane), F32)
        for j in range(n_lt):
            for k in range(CONV_WIDTH):
                wb_ref[j, k * nb:(k + 1) * nb, :] = jnp.broadcast_to(
                    dw_ref[k:k + 1, j * lane:(j + 1) * lane], (nb, lane))

    x = x_ref[...].reshape(nb * ts, d)
    h = _rms(x, g_ref[...]).astype(BF16)
    z = _dot(h, win_ref[...]) + bin_ref[...]
    dc = n_lt * lane
    glu = z[:, :dc] * _sigmoid(z[:, dc:])
    base = CONV_HALO - CONV_BUF
    for j in range(n_lt):
        lanes = slice(j * lane, (j + 1) * lane)
        for b in range(nb):
            glu_ref[j, b * pitch:b * pitch + ts, :] = glu[b * ts:(b + 1) * ts, lanes]
        for t in range(ts):
            ext_ref[j, (CONV_HALO + t) * nb:(CONV_HALO + t + 1) * nb, :] = (
                glu_ref[j, pl.ds(t, nb, stride=pitch), :])

        bias = jnp.broadcast_to(dwb_ref[:, lanes], (nb, lane))
        for t0 in range(0, ts, CONV_UNROLL):
            acc = [bias] * CONV_UNROLL
            for k0 in range(0, CONV_WIDTH, CONV_GROUP):
                n_taps = min(CONV_GROUP, CONV_WIDTH - k0)
                r0 = t0 + base + k0
                rows = [ext_ref[j, (r0 + m) * nb:(r0 + m + 1) * nb, :]
                        for m in range(CONV_UNROLL + n_taps - 1)]
                for kk in range(n_taps):
                    w = wb_ref[j, (k0 + kk) * nb:(k0 + kk + 1) * nb, :]
                    acc = [a + rows[u + kk] * w for u, a in enumerate(acc)]
            for u in range(CONV_UNROLL):
                c_ref[j, (t0 + u) * nb:(t0 + u + 1) * nb, :] = acc[u]

    c = jnp.concatenate(
        [jnp.concatenate([c_ref[j, pl.ds(b, ts, stride=nb), :] for b in range(nb)], axis=0)
         for j in range(n_lt)], axis=1)
    c = _silu(_ln(c, lng_ref[...], lnb_ref[...]))
    o_ref[...] = (x + _dot(c.astype(BF16), wout_ref[...])).reshape(nb, ts, d)

    @pl.when(s == pl.num_programs(0) - 1)
    def _():
        for b in range(nb):
            for j in range(n_lt):
                st_ref[b, :, j * lane:(j + 1) * lane] = (
                    glu_ref[j, b * pitch + ts - CONV_BUF:b * pitch + ts, :])

    ext_ref[:, 0:CONV_HALO * nb, :] = ext_ref[:, ts * nb:(ts + CONV_HALO) * nb, :]


def _conv_p(x, consts):
    nb, s, d = x.shape
    dc = consts[-1].shape[0]
    assert nb == V7X_SUBLANES, "the time-major copy puts one sequence on each sublane"
    ts = TM // nb
    assert s % ts == 0 and ts >= CONV_HALO
    return pl.pallas_call(
        functools.partial(_conv_p_kernel, nb=nb, ts=ts),
        grid=(s // ts,),
        in_specs=[pl.BlockSpec((nb, ts, d), lambda j: (0, j, 0))] + [_const_spec(c.shape) for c in consts],
        out_specs=[
            pl.BlockSpec((nb, ts, d), lambda j: (0, j, 0)),
            pl.BlockSpec((nb, CONV_BUF, dc), lambda j: (0, 0, 0)),
        ],
        out_shape=[
            jax.ShapeDtypeStruct((nb, s, d), F32),
            jax.ShapeDtypeStruct((nb, CONV_BUF, dc), F32),
        ],
        scratch_shapes=[
            pltpu.VMEM((dc // V7X_LANES, nb * (ts + V7X_SUBLANES), V7X_LANES), F32),
            pltpu.VMEM((dc // V7X_LANES, (CONV_HALO + ts) * nb, V7X_LANES), F32),
            pltpu.VMEM((dc // V7X_LANES, ts * nb, V7X_LANES), F32),
            pltpu.VMEM((dc // V7X_LANES, CONV_WIDTH * nb, V7X_LANES), F32),
        ],
        compiler_params=_params(1),
        name="conv_p",
    )(x, *consts)


def _gm_p_kernel(x_ref, g_ref, win_ref, lng_ref, lnb_ref, ws_ref, bs_ref, wout_ref,
                 o_ref, v_ref, *, tm):
    s = pl.program_id(1)
    dg = wout_ref.shape[0]
    grp = dg // N_GM_GROUPS
    row = lax.broadcasted_iota(jnp.int32, (CHUNK, CHUNK), 0)
    col = lax.broadcasted_iota(jnp.int32, (CHUNK, CHUNK), 1)
    causal = col <= row
    x = x_ref[...]
    h = _rms(x, g_ref[...]).astype(BF16)
    z = _gelu_tanh(_dot(h, win_ref[...]))
    u = z[:, :dg]
    v = _ln(z[:, dg:], lng_ref[...], lnb_ref[...])
    vb = v.astype(BF16)
    mixed_cols = []
    for gi in range(N_GM_GROUPS):
        wsg = jnp.where(causal, ws_ref[gi], 0.0).astype(BF16)
        bias = bs_ref[:, gi:gi + 1]
        parts = [_dot(wsg, vb[c0:c0 + CHUNK, gi * grp:(gi + 1) * grp]) + bias
                 for c0 in range(0, tm, CHUNK)]
        mixed_cols.append(jnp.concatenate(parts, axis=0))
    mixed = jnp.concatenate(mixed_cols, axis=1)
    y = (u * mixed).astype(BF16)
    o_ref[...] = x + _dot(y, wout_ref[...])

    @pl.when(s == pl.num_programs(1) - 1)
    def _():
        v_ref[...] = v[tm - CHUNK:, :]


def _sc_p_kernel(x_ref, g_ref, win_ref, cw_ref, wout_ref, o_ref, st_ref, ext_ref, *, tm):
    s = pl.program_id(1)
    d = wout_ref.shape[0]

    @pl.when(s == 0)
    def _():
        ext_ref[0:SC_HALO, :] = jnp.zeros((SC_HALO, d), F32)

    x = x_ref[...]
    h = _rms(x, g_ref[...]).astype(BF16)
    z = _dot(h, win_ref[...])
    cx = z[:, d:2 * d] * z[:, 2 * d:]
    ext_ref[SC_HALO:SC_HALO + tm, :] = cx
    conv = cx * cw_ref[SC_WIDTH - 1:SC_WIDTH, :]
    for k in range(SC_WIDTH - 1):
        off = SC_HALO - (SC_WIDTH - 1) + k
        conv = conv + ext_ref[off:off + tm, :] * cw_ref[k:k + 1, :]
    y = (z[:, :d] * conv).astype(BF16)
    o_ref[...] = x + _dot(y, wout_ref[...])

    @pl.when(s == pl.num_programs(1) - 1)
    def _():
        st_ref[...] = ext_ref[SC_HALO + tm - SC_BUF:SC_HALO + tm, :]

    _carry_halo(ext_ref, SC_HALO, tm)


def _prompt_mixer(kern, x, consts, *, state_rows, scratch, name, casts=()):
    b, s, d = x.shape
    tm = min(TM, s)
    n_j = s // tm
    cast_in, cast_out, cast_shape = _cast_specs(casts, b * n_j, lambda i, j: i * n_j + j)
    if casts:
        kern = functools.partial(kern, n_cast=len(casts))
    return pl.pallas_call(
        functools.partial(kern, tm=tm),
        grid=(b, n_j),
        in_specs=[pl.BlockSpec((None, tm, d), lambda i, j: (i, j, 0))]
        + [_const_spec(c.shape) for c in consts] + cast_in,
        out_specs=[
            pl.BlockSpec((None, tm, d), lambda i, j: (i, j, 0)),
            pl.BlockSpec((None, state_rows, d), lambda i, j: (i, 0, 0)),
        ] + cast_out,
        out_shape=[
            jax.ShapeDtypeStruct((b, s, d), F32),
            jax.ShapeDtypeStruct((b, state_rows, d), F32),
        ] + cast_shape,
        scratch_shapes=[pltpu.VMEM(shape, F32) for shape in scratch(tm, d)],
        compiler_params=_params(2),
        name=name,
    )(x, *consts, *[arr for arr, _ in casts])


def _slab(ref, j, d):
    return ref[:, j * d:(j + 1) * d]


def _pool_s_kernel(x_ref, st_ref, g_ref, w_ref, sc_ref, o_ref, nst_ref, d_ref, *, n_new, pos0):
    nb = x_ref.shape[1]
    grp = w_ref.shape[1]
    xs = [x_ref[l] for l in range(n_new)]
    hs = [_rms(x, g_ref[...]) for x in xs]

    def ext(j, c0):
        if j < POOL_BUF:
            return st_ref[j, :, c0:c0 + grp]
        return hs[j - POOL_BUF][:, c0:c0 + grp]

    for l in range(n_new):
        for gi, win in enumerate(POOL_WINDOWS):
            c0 = gi * grp
            acc = ext(POOL_BUF + l, c0)
            for k in range(1, win):
                acc = acc + ext(POOL_BUF + l - k, c0)
            cnt = float(min(win, pos0 + l + 1))
            d_ref[l * nb:(l + 1) * nb, c0:c0 + grp] = acc / cnt - hs[l][:, c0:c0 + grp]
    db = d_ref[...].astype(BF16)
    y = jnp.concatenate(
        [_dot(db[:, gi * grp:(gi + 1) * grp], w_ref[gi]) for gi in range(len(POOL_WINDOWS))], axis=1)
    y = y * sc_ref[...]
    for l in range(n_new):
        o_ref[l * nb:(l + 1) * nb, :] = xs[l] + y[l * nb:(l + 1) * nb, :]
    for j in range(POOL_BUF):
        src = j + n_new
        nst_ref[j] = st_ref[src] if src < POOL_BUF else hs[src - POOL_BUF]


def _conv_s_kernel(x_ref, st_hbm, g_ref, win_ref, bin_ref, dw_ref, dwb_ref, lng_ref, lnb_ref,
                   wout_ref, o_ref, nst_hbm, st_vmem, g_scr, c_scr, sem, *, n_new):
    nb = st_vmem.shape[1]
    dc = wout_ref.shape[0]
    keep = CONV_BUF - n_new
    load_old = pltpu.make_async_copy(st_hbm, st_vmem, sem.at[0])
    shift_old = pltpu.make_async_copy(
        st_vmem.at[pl.ds(n_new, keep)], nst_hbm.at[pl.ds(0, keep)], sem.at[1])
    store_new = pltpu.make_async_copy(g_scr, nst_hbm.at[pl.ds(keep, n_new)], sem.at[2])
    load_old.start()

    x = x_ref[...]
    h = _rms(x, g_ref[...]).astype(BF16)
    z = _dot(h, win_ref[...]) + bin_ref[...]
    glu = z[:, :dc] * _sigmoid(z[:, dc:])
    for l in range(n_new):
        g_scr[l] = glu[l * nb:(l + 1) * nb, :]
    store_new.start()
    load_old.wait()
    shift_old.start()

    def ext(j):
        return st_vmem[j] if j < CONV_BUF else g_scr[j - CONV_BUF]

    for l in range(n_new):
        acc = jnp.zeros((nb, dc), F32) + dwb_ref[...]
        for k in range(CONV_WIDTH):
            acc = acc + ext(l + k) * dw_ref[k:k + 1, :]
        c_scr[l * nb:(l + 1) * nb, :] = acc
    c = _silu(_ln(c_scr[...], lng_ref[...], lnb_ref[...]))
    o_ref[...] = x + _dot(c.astype(BF16), wout_ref[...])
    store_new.wait()
    shift_old.wait()


def _conv_s(xs, state, consts, *, n_new):
    m, d = xs.shape
    rows, nb, dc = state.shape
    assert rows == CONV_BUF and m == n_new * nb and 0 < n_new < CONV_BUF
    vmem = pl.BlockSpec(memory_space=pltpu.VMEM)
    hbm = pl.BlockSpec(memory_space=pl.ANY)
    return pl.pallas_call(
        functools.partial(_conv_s_kernel, n_new=n_new),
        in_specs=[vmem, hbm] + [vmem] * len(consts),
        out_specs=[vmem, hbm],
        out_shape=[jax.ShapeDtypeStruct((m, d), F32), jax.ShapeDtypeStruct(state.shape, F32)],
        scratch_shapes=[
            pltpu.VMEM(state.shape, F32),
            pltpu.VMEM((n_new, nb, dc), F32),
            pltpu.VMEM((m, dc), F32),
            pltpu.SemaphoreType.DMA((3,)),
        ],
        compiler_params=pltpu.CompilerParams(vmem_limit_bytes=VMEM_LIMIT),
        name="conv_s",
    )(xs, state, *consts)


def _gm_s_kernel(x_ref, g_ref, win_ref, lng_ref, lnb_ref, wrow_ref, brow_ref, wout_ref,
                 o_ref, v_ref, *, n_new, nb):
    dg = wout_ref.shape[0]
    x = x_ref[...]
    h = _rms(x, g_ref[...]).astype(BF16)
    z = _gelu_tanh(_dot(h, win_ref[...]))
    u = z[:, :dg]
    v = _ln(z[:, dg:], lng_ref[...], lnb_ref[...])
    mixed = []
    for t in range(n_new):
        acc = jnp.zeros((nb, dg), F32) + brow_ref[t:t + 1, :]
        for s in range(t + 1):
            acc = acc + wrow_ref[t * n_new + s:t * n_new + s + 1, :] * v[s * nb:(s + 1) * nb, :]
        mixed.append(acc)
    y = (u * jnp.concatenate(mixed, axis=0)).astype(BF16)
    o_ref[...] = x + _dot(y, wout_ref[...])
    for l in range(n_new):
        v_ref[:, l * dg:(l + 1) * dg] = v[l * nb:(l + 1) * nb, :]


def _sc_s_kernel(x_ref, st_ref, g_ref, win_ref, cw_ref, wout_ref, o_ref, nst_ref, *, n_new):
    nb = st_ref.shape[0]
    d = wout_ref.shape[0]
    x = x_ref[...]
    h = _rms(x, g_ref[...]).astype(BF16)
    z = _dot(h, win_ref[...])
    cx = z[:, d:2 * d] * z[:, 2 * d:]

    def ext(j):
        if j < SC_BUF:
            return _slab(st_ref, j, d)
        return cx[(j - SC_BUF) * nb:(j - SC_BUF + 1) * nb, :]

    convs = []
    for l in range(n_new):
        acc = ext(l) * cw_ref[0:1, :]
        for k in range(1, SC_WIDTH):
            acc = acc + ext(l + k) * cw_ref[k:k + 1, :]
        convs.append(acc)
    y = (z[:, :d] * jnp.concatenate(convs, axis=0)).astype(BF16)
    o_ref[...] = x + _dot(y, wout_ref[...])
    for j in range(SC_BUF):
        nst_ref[:, j * d:(j + 1) * d] = ext(j + n_new)


def _single_step(kern, operands, out_shape, scratch, name):
    return pl.pallas_call(
        kern,
        out_shape=out_shape,
        scratch_shapes=scratch,
        compiler_params=pltpu.CompilerParams(vmem_limit_bytes=VMEM_LIMIT),
        name=name,
    )(*operands)


def kernel(x_prompt, x_sample, state_pool, state_conv, state_shortconv, norm_mix, norm_ffn, norm_final, pool_w, pool_scale, conv_w_in, conv_b_in, conv_dw, conv_dw_b, conv_ln_g, conv_ln_b, conv_w_out, gm_w_in, gm_ln_g, gm_ln_b, gm_w_s, gm_b_s, gm_w_out, sc_w_in, sc_conv, sc_w_out, ffn_w_in, ffn_w_out):
    bp, sp, d = x_prompt.shape
    nb, n_new, _ = x_sample.shape
    depth = norm_mix.shape[0]
    assert depth == 4 and pool_w.shape[0] == 1, "one layer of each mixer type"
    assert sp % CHUNK == 0 and n_new < CHUNK and PAST_LEN % CHUNK == 0

    row = lambda a: a.reshape(1, -1)
    bf = lambda a: a.astype(BF16)
    rows_major = lambda a: jnp.transpose(a, (1, 0, 2))
    gfin = row(norm_final)
    gffn = norm_ffn.reshape(depth, 1, d)
    grp = d // N_GM_GROUPS
    m = nb * n_new
    tm_shape = jax.ShapeDtypeStruct((m, d), F32)

    def ffn(x, xs, i, w_in, w_out, casts):
        out = _ffn(x.reshape(bp * sp, d), xs, gffn, w_in, w_out, gfin, casts,
                   layer=i, final=(i == depth - 1))
        return out[0].reshape(bp, sp, d), out[1], out[2:]

    pool_consts = (row(norm_mix[0]), bf(pool_w[0]), row(pool_scale[0]))
    x, pool_p, f_in, f_out = _prompt_mixer(
        functools.partial(_pool_p_kernel, pos0=0), x_prompt, pool_consts, state_rows=POOL_BUF,
        scratch=lambda tm, dd: [(POOL_HALO + tm, dd), (2, POOL_HALO + tm, dd // len(POOL_WINDOWS))],
        name="pool_p", casts=[(ffn_w_in, 0), (ffn_w_out, 0)])
    xs, pool_s = _single_step(
        functools.partial(_pool_s_kernel, n_new=n_new, pos0=PAST_LEN),
        (rows_major(x_sample), rows_major(state_pool[0])) + pool_consts,
        [tm_shape, jax.ShapeDtypeStruct((POOL_BUF, nb, d), F32)],
        [pltpu.VMEM((m, d), F32)], "pool_s")
    x, xs, (cw_in, cw_out, f_in, f_out) = ffn(
        x, xs, 0, f_in, f_out, [(conv_w_in, 0), (conv_w_out, 0), (ffn_w_in, 1), (ffn_w_out, 1)])

    conv_consts = (row(norm_mix[1]), cw_in, row(conv_b_in[0]), conv_dw[0], row(conv_dw_b[0]),
                   row(conv_ln_g[0]), row(conv_ln_b[0]), cw_out)
    x, conv_p = _conv_p(x, conv_consts)
    xs, conv_s = _conv_s(xs, rows_major(state_conv[0]), conv_consts, n_new=n_new)
    x, xs, (gw_in, gw_out, f_in, f_out) = ffn(
        x, xs, 1, f_in, f_out, [(gm_w_in, 0), (gm_w_out, 0), (ffn_w_in, 2), (ffn_w_out, 2)])

    gm_head = (row(norm_mix[2]), gw_in, row(gm_ln_g[0]), row(gm_ln_b[0]))
    x, v_p = _prompt_mixer(
        _gm_p_kernel, x, gm_head + (gm_w_s[0], gm_b_s[0].T, gw_out), state_rows=CHUNK,
        scratch=lambda tm, dd: [], name="gm_p")
    wrow = jnp.repeat(jnp.transpose(gm_w_s[0][:, :n_new, :n_new], (1, 2, 0)).reshape(n_new * n_new, N_GM_GROUPS), grp, axis=1)
    brow = jnp.repeat(gm_b_s[0][:, :n_new].T, grp, axis=1)
    xs, v_s = _single_step(
        functools.partial(_gm_s_kernel, n_new=n_new, nb=nb),
        (xs,) + gm_head + (wrow, brow, gw_out),
        [tm_shape, jax.ShapeDtypeStruct((nb, n_new * d), F32)],
        [], "gm_s")
    x, xs, (sw_in, sw_out, f_in, f_out) = ffn(
        x, xs, 2, f_in, f_out, [(sc_w_in, 0), (sc_w_out, 0), (ffn_w_in, 3), (ffn_w_out, 3)])

    sc_consts = (row(norm_mix[3]), sw_in, sc_conv[0], sw_out)
    x, sc_p = _prompt_mixer(
        _sc_p_kernel, x, sc_consts, state_rows=SC_BUF,
        scratch=lambda tm, dd: [(SC_HALO + tm, dd)], name="sc_p")
    xs, sc_s = _single_step(
        functools.partial(_sc_s_kernel, n_new=n_new),
        (xs, state_shortconv[0].reshape(nb, SC_BUF * d)) + sc_consts,
        [tm_shape, jax.ShapeDtypeStruct((nb, SC_BUF * d), F32)],
        [], "sc_s")
    y_prompt, ys, _ = ffn(x, xs, 3, f_in, f_out, [])
    y_sample = jnp.transpose(ys.reshape(n_new, nb, d), (1, 0, 2))

    return (y_prompt, y_sample,
            pool_p[None], rows_major(pool_s)[None],
            conv_p[None], rows_major(conv_s)[None],
            v_p[None], v_s.reshape(1, nb, n_new, d),
            sc_p[None], sc_s.reshape(1, nb, SC_BUF, d))
```

```python
import functools

import jax
import jax.numpy as jnp
from jax import lax
from jax.experimental import pallas as pl
from jax.experimental.pallas import tpu as pltpu

F32 = jnp.float32
BF16 = jnp.bfloat16

EPS = 1e-6
POOL_WINDOWS = (2, 4, 8, 16)
POOL_BUF = max(POOL_WINDOWS) - 1
CONV_WIDTH = 31
CONV_BUF = CONV_WIDTH - 1
CHUNK = 128
N_GM_GROUPS = 4
SC_WIDTH = 3
SC_BUF = SC_WIDTH - 1
PAST_LEN = 16384

V7X_SUBLANES = 8
V7X_BF16_SUBLANES = 16
V7X_LANES = 128
V7X_MXU_DIM = 256
V7X_VMEM_BYTES = 64 * 1024 * 1024
VMEM_LIMIT = V7X_VMEM_BYTES - 4 * 1024 * 1024

TM = 1024
POOL_HALO = 32
CONV_HALO = 32
SC_HALO = 8
CONV_UNROLL = 16
CONV_GROUP = 8


def _params(n_grid):
    return pltpu.CompilerParams(
        dimension_semantics=("arbitrary",) * n_grid, vmem_limit_bytes=VMEM_LIMIT)


def _const_spec(shape):
    nd = len(shape)
    return pl.BlockSpec(shape, lambda *_: (0,) * nd, pipeline_mode=pl.Buffered(1))


def _rms(x, g):
    y = x * lax.rsqrt(jnp.mean(x * x, axis=-1, keepdims=True) + EPS)
    return y * g


def _ln(x, g, b):
    mu = jnp.mean(x, axis=-1, keepdims=True)
    xc = x - mu
    y = xc * lax.rsqrt(jnp.mean(xc * xc, axis=-1, keepdims=True) + EPS)
    return y * g + b


def _sigmoid(x):
    return 0.5 * (1.0 + jnp.tanh(0.5 * x))


def _silu(x):
    return x * _sigmoid(x)


def _gelu_tanh(x):
    c = 0.7978845608028654
    return x * (0.5 * (1.0 + jnp.tanh(c * (x + 0.044715 * (x * x * x)))))


def _dot(a, b):
    return jnp.dot(a, b, preferred_element_type=F32)


def _ffn_chunks(d_ff, n_chunks):
    tiles = d_ff // V7X_MXU_DIM
    assert tiles * V7X_MXU_DIM == d_ff
    edges = [V7X_MXU_DIM * ((tiles * i + n_chunks - 1) // n_chunks) for i in range(n_chunks + 1)]
    return list(zip(edges[:-1], edges[1:]))


def _ffn_kernel(*refs, d_ff, chunks, final, n_cast, n_prompt_steps):
    xp_ref, xs_ref, g_ref, win_ref, wout_ref, gf_ref = refs[:6]
    cast_src = refs[6:6 + n_cast]
    op_ref, os_ref = refs[6 + n_cast:8 + n_cast]
    cast_dst = refs[8 + n_cast:]
    i = pl.program_id(0)

    def rows(x_ref, o_ref):
        x = x_ref[...]
        h = _rms(x, g_ref[...]).astype(BF16)
        acc = x
        for c0, c1 in chunks:
            gate = _dot(h, win_ref[:, c0:c1])
            up = _dot(h, win_ref[:, d_ff + c0:d_ff + c1])
            act = (_silu(gate) * up).astype(BF16)
            acc = acc + _dot(act, wout_ref[c0:c1, :])
        if final:
            acc = _rms(acc, gf_ref[...])
        o_ref[...] = acc

    @pl.when(i < n_prompt_steps)
    def _():
        rows(xp_ref, op_ref)
        for src, dst in zip(cast_src, cast_dst):
            dst[...] = src[...].astype(BF16)

    @pl.when(i == n_prompt_steps)
    def _():
        rows(xs_ref, os_ref)


def _layer_spec(shape, layer):
    nd = len(shape)
    return pl.BlockSpec((None,) + tuple(shape[1:]), lambda *_: (layer,) + (0,) * (nd - 1),
                        pipeline_mode=pl.Buffered(1))


def _cast_specs(casts, n_blocks, block_of):
    cast_in, cast_out, cast_shape = [], [], []
    for arr, idx in casts:
        _, r, c = arr.shape
        assert r % (n_blocks * V7X_BF16_SUBLANES) == 0
        cast_in.append(pl.BlockSpec((None, r // n_blocks, c),
                                    lambda *g, idx=idx: (idx, block_of(*g), 0)))
        cast_out.append(pl.BlockSpec((r // n_blocks, c), lambda *g: (block_of(*g), 0)))
        cast_shape.append(jax.ShapeDtypeStruct((r, c), BF16))
    return cast_in, cast_out, cast_shape


def _ffn(xp, xs, g, w_in, w_out, g_final, casts, *, layer, final):
    mp, d = xp.shape
    ms = xs.shape[0]
    d_ff = w_out.shape[0]
    assert mp % TM == 0
    n_steps = mp // TM
    last = n_steps - 1
    kern = functools.partial(_ffn_kernel, d_ff=d_ff, chunks=_ffn_chunks(d_ff, 2), final=final,
                             n_cast=len(casts), n_prompt_steps=n_steps)
    cast_in, cast_out, cast_shape = _cast_specs(casts, n_steps, lambda i: jnp.minimum(i, last))
    return pl.pallas_call(
        kern,
        grid=(n_steps + 1,),
        in_specs=[
            pl.BlockSpec((TM, d), lambda i: (jnp.minimum(i, last), 0)),
            _const_spec((ms, d)),
            _layer_spec(g.shape, layer),
            _const_spec(w_in.shape),
            _const_spec(w_out.shape),
            _const_spec((1, d)),
        ] + cast_in,
        out_specs=[
            pl.BlockSpec((TM, d), lambda i: (jnp.minimum(i, last), 0)),
            pl.BlockSpec((ms, d), lambda i: (0, 0)),
        ] + cast_out,
        out_shape=[jax.ShapeDtypeStruct((mp, d), F32), jax.ShapeDtypeStruct((ms, d), F32)] + cast_shape,
        compiler_params=_params(1),
        name="ffn_final" if final else "ffn",
    )(xp, xs, g, w_in, w_out, g_final, *[arr for arr, _ in casts])


def _carry_halo(ext_ref, halo, tm):
    ext_ref[0:halo, :] = ext_ref[tm:tm + halo, :]


def _pool_p_kernel(*refs, tm, pos0, n_cast):
    x_ref, g_ref, w_ref, sc_ref = refs[:4]
    cast_src = refs[4:4 + n_cast]
    o_ref, st_ref = refs[4 + n_cast:6 + n_cast]
    cast_dst = refs[6 + n_cast:6 + 2 * n_cast]
    ext_ref, tmp_ref = refs[6 + 2 * n_cast:]
    s = pl.program_id(1)
    grp = w_ref.shape[1]
    halo = POOL_HALO
    end = halo + tm

    @pl.when(s == 0)
    def _():
        ext_ref[0:halo, :] = jnp.zeros((halo, ext_ref.shape[1]), F32)

    x = x_ref[...]
    h = _rms(x, g_ref[...])
    ext_ref[halo:end, :] = h
    pos = pos0 + s * tm + lax.broadcasted_iota(jnp.int32, (tm, 1), 0)
    ys = []
    for gi, win in enumerate(POOL_WINDOWS):
        lanes = slice(gi * grp, (gi + 1) * grp)
        assert win == 1 << (win.bit_length() - 1) and V7X_SUBLANES * (win.bit_length() - 1) <= halo
        read = lambda r0, r1: ext_ref[r0:r1, lanes]
        shift, level = 1, 1
        while 2 * shift < win:
            r0 = V7X_SUBLANES * level
            slot = level % 2
            tmp_ref[slot, r0:end, :] = read(r0, end) + read(r0 - shift, end - shift)
            read = lambda r0, r1, slot=slot: tmp_ref[slot, r0:r1, :]
            shift, level = 2 * shift, level + 1
        acc = read(halo, end) + read(halo - shift, end - shift)
        inv_cnt = 1.0 / jnp.minimum(win, pos + 1).astype(F32)
        d = acc * inv_cnt - h[:, lanes]
        ys.append(_dot(d.astype(BF16), w_ref[gi]))
    y = jnp.concatenate(ys, axis=1)
    o_ref[...] = x + y * sc_ref[...]
    for src, dst in zip(cast_src, cast_dst):
        dst[...] = src[...].astype(BF16)

    @pl.when(s == pl.num_programs(1) - 1)
    def _():
        st_ref[...] = ext_ref[end - POOL_BUF:end, :]

    _carry_halo(ext_ref, halo, tm)


def _conv_p_kernel(x_ref, g_ref, win_ref, bin_ref, dw_ref, dwb_ref, lng_ref, lnb_ref, wout_ref,
                   o_ref, st_ref, glu_ref, ext_ref, c_ref, wb_ref, *, nb, ts):
    s = pl.program_id(0)
    d = x_ref.shape[2]
    n_lt, _, lane = glu_ref.shape
    pitch = glu_ref.shape[1] // nb

    @pl.when(s == 0)
    def _():
        ext_ref[:, 0:CONV_HALO * nb, :] = jnp.zeros((n_lt, CONV_HALO * nb, lane), F32)
        for j in range(n_lt):
            for k in range(CONV_WIDTH):
                wb_ref[j, k * nb:(k + 1) * nb, :] = jnp.broadcast_to(
                    dw_ref[k:k + 1, j * lane:(j + 1) * lane], (nb, lane))

    x = x_ref[...].reshape(nb * ts, d)
    h = _rms(x, g_ref[...]).astype(BF16)
    z = _dot(h, win_ref[...]) + bin_ref[...]
    dc = n_lt * lane
    glu = z[:, :dc] * _sigmoid(z[:, dc:])
    base = CONV_HALO - CONV_BUF
    for j in range(n_lt):
        lanes = slice(j * lane, (j + 1) * lane)
        for b in range(nb):
            glu_ref[j, b * pitch:b * pitch + ts, :] = glu[b * ts:(b + 1) * ts, lanes]
        for t in range(ts):
            ext_ref[j, (CONV_HALO + t) * nb:(CONV_HALO + t + 1) * nb, :] = (
                glu_ref[j, pl.ds(t, nb, stride=pitch), :])

        bias = jnp.broadcast_to(dwb_ref[:, lanes], (nb, lane))
        for t0 in range(0, ts, CONV_UNROLL):
            acc = [bias] * CONV_UNROLL
            for k0 in range(0, CONV_WIDTH, CONV_GROUP):
                n_taps = min(CONV_GROUP, CONV_WIDTH - k0)
                r0 = t0 + base + k0
                rows = [ext_ref[j, (r0 + m) * nb:(r0 + m + 1) * nb, :]
                        for m in range(CONV_UNROLL + n_taps - 1)]
                for kk in range(n_taps):
                    w = wb_ref[j, (k0 + kk) * nb:(k0 + kk + 1) * nb, :]
                    acc = [a + rows[u + kk] * w for u, a in enumerate(acc)]
            for u in range(CONV_UNROLL):
                c_ref[j, (t0 + u) * nb:(t0 + u + 1) * nb, :] = acc[u]

    c = jnp.concatenate(
        [jnp.concatenate([c_ref[j, pl.ds(b, ts, stride=nb), :] for b in range(nb)], axis=0)
         for j in range(n_lt)], axis=1)
    c = _silu(_ln(c, lng_ref[...], lnb_ref[...]))
    o_ref[...] = (x + _dot(c.astype(BF16), wout_ref[...])).reshape(nb, ts, d)

    @pl.when(s == pl.num_programs(0) - 1)
    def _():
        for b in range(nb):
            for j in range(n_lt):
                st_ref[b, :, j * lane:(j + 1) * lane] = (
                    glu_ref[j, b * pitch + ts - CONV_BUF:b * pitch + ts, :])

    ext_ref[:, 0:CONV_HALO * nb, :] = ext_ref[:, ts * nb:(ts + CONV_HALO) * nb, :]


def _conv_p(x, consts):
    nb, s, d = x.shape
    dc = consts[-1].shape[0]
    assert nb == V7X_SUBLANES, "the time-major copy puts one sequence on each sublane"
    ts = TM // nb
    assert s % ts == 0 and ts >= CONV_HALO
    return pl.pallas_call(
        functools.partial(_conv_p_kernel, nb=nb, ts=ts),
        grid=(s // ts,),
        in_specs=[pl.BlockSpec((nb, ts, d), lambda j: (0, j, 0))] + [_const_spec(c.shape) for c in consts],
        out_specs=[
            pl.BlockSpec((nb, ts, d), lambda j: (0, j, 0)),
            pl.BlockSpec((nb, CONV_BUF, dc), lambda j: (0, 0, 0)),
        ],
        out_shape=[
            jax.ShapeDtypeStruct((nb, s, d), F32),
            jax.ShapeDtypeStruct((nb, CONV_BUF, dc), F32),
        ],
        scratch_shapes=[
            pltpu.VMEM((dc // V7X_LANES, nb * (ts + V7X_SUBLANES), V7X_LANES), F32),
            pltpu.VMEM((dc // V7X_LANES, (CONV_HALO + ts) * nb, V7X_LANES), F32),
            pltpu.VMEM((dc // V7X_LANES, ts * nb, V7X_LANES), F32),
            pltpu.VMEM((dc // V7X_LANES, CONV_WIDTH * nb, V7X_LANES), F32),
        ],
        compiler_params=_params(1),
        name="conv_p",
    )(x, *consts)


def _gm_p_kernel(x_ref, g_ref, win_ref, lng_ref, lnb_ref, ws_ref, bs_ref, wout_ref,
                 o_ref, v_ref, *, tm):
    s = pl.program_id(1)
    dg = wout_ref.shape[0]
    grp = dg // N_GM_GROUPS
    row = lax.broadcasted_iota(jnp.int32, (CHUNK, CHUNK), 0)
    col = lax.broadcasted_iota(jnp.int32, (CHUNK, CHUNK), 1)
    causal = col <= row
    x = x_ref[...]
    h = _rms(x, g_ref[...]).astype(BF16)
    z = _gelu_tanh(_dot(h, win_ref[...]))
    u = z[:, :dg]
    v = _ln(z[:, dg:], lng_ref[...], lnb_ref[...])
    vb = v.astype(BF16)
    mixed_cols = []
    for gi in range(N_GM_GROUPS):
        wsg = jnp.where(causal, ws_ref[gi], 0.0).astype(BF16)
        bias = bs_ref[:, gi:gi + 1]
        parts = [_dot(wsg, vb[c0:c0 + CHUNK, gi * grp:(gi + 1) * grp]) + bias
                 for c0 in range(0, tm, CHUNK)]
        mixed_cols.append(jnp.concatenate(parts, axis=0))
    mixed = jnp.concatenate(mixed_cols, axis=1)
    y = (u * mixed).astype(BF16)
    o_ref[...] = x + _dot(y, wout_ref[...])

    @pl.when(s == pl.num_programs(1) - 1)
    def _():
        v_ref[...] = v[tm - CHUNK:, :]


def _sc_p_kernel(x_ref, g_ref, win_ref, cw_ref, wout_ref, o_ref, st_ref, ext_ref, *, tm):
    s = pl.program_id(1)
    d = wout_ref.shape[0]

    @pl.when(s == 0)
    def _():
        ext_ref[0:SC_HALO, :] = jnp.zeros((SC_HALO, d), F32)

    x = x_ref[...]
    h = _rms(x, g_ref[...]).astype(BF16)
    z = _dot(h, win_ref[...])
    cx = z[:, d:2 * d] * z[:, 2 * d:]
    ext_ref[SC_HALO:SC_HALO + tm, :] = cx
    conv = cx * cw_ref[SC_WIDTH - 1:SC_WIDTH, :]
    for k in range(SC_WIDTH - 1):
        off = SC_HALO - (SC_WIDTH - 1) + k
        conv = conv + ext_ref[off:off + tm, :] * cw_ref[k:k + 1, :]
    y = (z[:, :d] * conv).astype(BF16)
    o_ref[...] = x + _dot(y, wout_ref[...])

    @pl.when(s == pl.num_programs(1) - 1)
    def _():
        st_ref[...] = ext_ref[SC_HALO + tm - SC_BUF:SC_HALO + tm, :]

    _carry_halo(ext_ref, SC_HALO, tm)


def _prompt_mixer(kern, x, consts, *, state_rows, scratch, name, casts=()):
    b, s, d = x.shape
    tm = min(TM, s)
    n_j = s // tm
    cast_in, cast_out, cast_shape = _cast_specs(casts, b * n_j, lambda i, j: i * n_j + j)
    if casts:
        kern = functools.partial(kern, n_cast=len(casts))
    return pl.pallas_call(
        functools.partial(kern, tm=tm),
        grid=(b, n_j),
        in_specs=[pl.BlockSpec((None, tm, d), lambda i, j: (i, j, 0))]
        + [_const_spec(c.shape) for c in consts] + cast_in,
        out_specs=[
            pl.BlockSpec((None, tm, d), lambda i, j: (i, j, 0)),
            pl.BlockSpec((None, state_rows, d), lambda i, j: (i, 0, 0)),
        ] + cast_out,
        out_shape=[
            jax.ShapeDtypeStruct((b, s, d), F32),
            jax.ShapeDtypeStruct((b, state_rows, d), F32),
        ] + cast_shape,
        scratch_shapes=[pltpu.VMEM(shape, F32) for shape in scratch(tm, d)],
        compiler_params=_params(2),
        name=name,
    )(x, *consts, *[arr for arr, _ in casts])


def _slab(ref, j, d):
    return ref[:, j * d:(j + 1) * d]


def _pool_s_kernel(x_ref, st_hbm, g_ref, w_ref, sc_ref, o_ref, nst_hbm, st_vmem, h_scr, d_ref, sem,
                   *, n_new, pos0):
    nb = x_ref.shape[1]
    grp = w_ref.shape[1]
    keep = POOL_BUF - n_new
    load_old = pltpu.make_async_copy(st_hbm, st_vmem, sem.at[0])
    shift_old = pltpu.make_async_copy(
        st_vmem.at[pl.ds(n_new, keep)], nst_hbm.at[pl.ds(0, keep)], sem.at[1])
    store_new = pltpu.make_async_copy(h_scr, nst_hbm.at[pl.ds(keep, n_new)], sem.at[2])
    load_old.start()
    xs = [x_ref[l] for l in range(n_new)]
    hs = [_rms(x, g_ref[...]) for x in xs]
    for l in range(n_new):
        h_scr[l] = hs[l]
    store_new.start()
    load_old.wait()
    shift_old.start()

    def ext(j, c0):
        if j < POOL_BUF:
            return st_vmem[j, :, c0:c0 + grp]
        return hs[j - POOL_BUF][:, c0:c0 + grp]

    for l in range(n_new):
        for gi, win in enumerate(POOL_WINDOWS):
            c0 = gi * grp
            acc = ext(POOL_BUF + l, c0)
            for k in range(1, win):
                acc = acc + ext(POOL_BUF + l - k, c0)
            cnt = float(min(win, pos0 + l + 1))
            d_ref[l * nb:(l + 1) * nb, c0:c0 + grp] = acc / cnt - hs[l][:, c0:c0 + grp]
    db = d_ref[...].astype(BF16)
    y = jnp.concatenate(
        [_dot(db[:, gi * grp:(gi + 1) * grp], w_ref[gi]) for gi in range(len(POOL_WINDOWS))], axis=1)
    y = y * sc_ref[...]
    for l in range(n_new):
        o_ref[l * nb:(l + 1) * nb, :] = xs[l] + y[l * nb:(l + 1) * nb, :]
    store_new.wait()
    shift_old.wait()


def _pool_s(x, state, consts, *, n_new, pos0):
    _, nb, d = x.shape
    assert state.shape == (POOL_BUF, nb, d) and 0 < n_new < POOL_BUF
    vmem = pl.BlockSpec(memory_space=pltpu.VMEM)
    hbm = pl.BlockSpec(memory_space=pl.ANY)
    return pl.pallas_call(
        functools.partial(_pool_s_kernel, n_new=n_new, pos0=pos0),
        in_specs=[vmem, hbm] + [vmem] * len(consts),
        out_specs=[vmem, hbm],
        out_shape=[jax.ShapeDtypeStruct((n_new * nb, d), F32), jax.ShapeDtypeStruct(state.shape, F32)],
        scratch_shapes=[
            pltpu.VMEM(state.shape, F32),
            pltpu.VMEM((n_new, nb, d), F32),
            pltpu.VMEM((n_new * nb, d), F32),
            pltpu.SemaphoreType.DMA((3,)),
        ],
        compiler_params=pltpu.CompilerParams(vmem_limit_bytes=VMEM_LIMIT),
        name="pool_s",
    )(x, state, *consts)


def _conv_s_kernel(x_ref, st_hbm, g_ref, win_ref, bin_ref, dw_ref, dwb_ref, lng_ref, lnb_ref,
                   wout_ref, o_ref, nst_hbm, st_vmem, g_scr, c_scr, sem, *, n_new):
    nb = st_vmem.shape[1]
    dc = wout_ref.shape[0]
    keep = CONV_BUF - n_new
    load_old = pltpu.make_async_copy(st_hbm, st_vmem, sem.at[0])
    shift_old = pltpu.make_async_copy(
        st_vmem.at[pl.ds(n_new, keep)], nst_hbm.at[pl.ds(0, keep)], sem.at[1])
    store_new = pltpu.make_async_copy(g_scr, nst_hbm.at[pl.ds(keep, n_new)], sem.at[2])
    load_old.start()

    x = x_ref[...]
    h = _rms(x, g_ref[...]).astype(BF16)
    z = _dot(h, win_ref[...]) + bin_ref[...]
    glu = z[:, :dc] * _sigmoid(z[:, dc:])
    for l in range(n_new):
        g_scr[l] = glu[l * nb:(l + 1) * nb, :]
    store_new.start()
    load_old.wait()
    shift_old.start()

    def ext(j):
        return st_vmem[j] if j < CONV_BUF else g_scr[j - CONV_BUF]

    for l in range(n_new):
        acc = jnp.zeros((nb, dc), F32) + dwb_ref[...]
        for k in range(CONV_WIDTH):
            acc = acc + ext(l + k) * dw_ref[k:k + 1, :]
        c_scr[l * nb:(l + 1) * nb, :] = acc
    c = _silu(_ln(c_scr[...], lng_ref[...], lnb_ref[...]))
    o_ref[...] = x + _dot(c.astype(BF16), wout_ref[...])
    store_new.wait()
    shift_old.wait()


def _conv_s(xs, state, consts, *, n_new):
    m, d = xs.shape
    rows, nb, dc = state.shape
    assert rows == CONV_BUF and m == n_new * nb and 0 < n_new < CONV_BUF
    vmem = pl.BlockSpec(memory_space=pltpu.VMEM)
    hbm = pl.BlockSpec(memory_space=pl.ANY)
    return pl.pallas_call(
        functools.partial(_conv_s_kernel, n_new=n_new),
        in_specs=[vmem, hbm] + [vmem] * len(consts),
        out_specs=[vmem, hbm],
        out_shape=[jax.ShapeDtypeStruct((m, d), F32), jax.ShapeDtypeStruct(state.shape, F32)],
        scratch_shapes=[
            pltpu.VMEM(state.shape, F32),
            pltpu.VMEM((n_new, nb, dc), F32),
            pltpu.VMEM((m, dc), F32),
            pltpu.SemaphoreType.DMA((3,)),
        ],
        compiler_params=pltpu.CompilerParams(vmem_limit_bytes=VMEM_LIMIT),
        name="conv_s",
    )(xs, state, *consts)


def _gm_s_kernel(x_ref, g_ref, win_ref, lng_ref, lnb_ref, wrow_ref, brow_ref, wout_ref,
                 o_ref, v_ref, *, n_new, nb):
    dg = wout_ref.shape[0]
    x = x_ref[...]
    h = _rms(x, g_ref[...]).astype(BF16)
    z = _gelu_tanh(_dot(h, win_ref[...]))
    u = z[:, :dg]
    v = _ln(z[:, dg:], lng_ref[...], lnb_ref[...])
    mixed = []
    for t in range(n_new):
        acc = jnp.zeros((nb, dg), F32) + brow_ref[t:t + 1, :]
        for s in range(t + 1):
            acc = acc + wrow_ref[t * n_new + s:t * n_new + s + 1, :] * v[s * nb:(s + 1) * nb, :]
        mixed.append(acc)
    y = (u * jnp.concatenate(mixed, axis=0)).astype(BF16)
    o_ref[...] = x + _dot(y, wout_ref[...])
    for l in range(n_new):
        v_ref[:, l * dg:(l + 1) * dg] = v[l * nb:(l + 1) * nb, :]


def _sc_s_kernel(x_ref, st_ref, g_ref, win_ref, cw_ref, wout_ref, o_ref, nst_ref, *, n_new):
    nb = st_ref.shape[0]
    d = wout_ref.shape[0]
    x = x_ref[...]
    h = _rms(x, g_ref[...]).astype(BF16)
    z = _dot(h, win_ref[...])
    cx = z[:, d:2 * d] * z[:, 2 * d:]

    def ext(j):
        if j < SC_BUF:
            return _slab(st_ref, j, d)
        return cx[(j - SC_BUF) * nb:(j - SC_BUF + 1) * nb, :]

    convs = []
    for l in range(n_new):
        acc = ext(l) * cw_ref[0:1, :]
        for k in range(1, SC_WIDTH):
            acc = acc + ext(l + k) * cw_ref[k:k + 1, :]
        convs.append(acc)
    y = (z[:, :d] * jnp.concatenate(convs, axis=0)).astype(BF16)
    o_ref[...] = x + _dot(y, wout_ref[...])
    for j in range(SC_BUF):
        nst_ref[:, j * d:(j + 1) * d] = ext(j + n_new)


def _single_step(kern, operands, out_shape, scratch, name):
    return pl.pallas_call(
        kern,
        out_shape=out_shape,
        scratch_shapes=scratch,
        compiler_params=pltpu.CompilerParams(vmem_limit_bytes=VMEM_LIMIT),
        name=name,
    )(*operands)


def kernel(x_prompt, x_sample, state_pool, state_conv, state_shortconv, norm_mix, norm_ffn, norm_final, pool_w, pool_scale, conv_w_in, conv_b_in, conv_dw, conv_dw_b, conv_ln_g, conv_ln_b, conv_w_out, gm_w_in, gm_ln_g, gm_ln_b, gm_w_s, gm_b_s, gm_w_out, sc_w_in, sc_conv, sc_w_out, ffn_w_in, ffn_w_out):
    bp, sp, d = x_prompt.shape
    nb, n_new, _ = x_sample.shape
    depth = norm_mix.shape[0]
    assert depth == 4 and pool_w.shape[0] == 1, "one layer of each mixer type"
    assert sp % CHUNK == 0 and n_new < CHUNK and PAST_LEN % CHUNK == 0

    row = lambda a: a.reshape(1, -1)
    bf = lambda a: a.astype(BF16)
    rows_major = lambda a: jnp.transpose(a, (1, 0, 2))
    gfin = row(norm_final)
    gffn = norm_ffn.reshape(depth, 1, d)
    grp = d // N_GM_GROUPS
    m = nb * n_new
    tm_shape = jax.ShapeDtypeStruct((m, d), F32)

    def ffn(x, xs, i, w_in, w_out, casts):
        out = _ffn(x.reshape(bp * sp, d), xs, gffn, w_in, w_out, gfin, casts,
                   layer=i, final=(i == depth - 1))
        return out[0].reshape(bp, sp, d), out[1], out[2:]

    pool_consts = (row(norm_mix[0]), bf(pool_w[0]), row(pool_scale[0]))
    x, pool_p, f_in, f_out = _prompt_mixer(
        functools.partial(_pool_p_kernel, pos0=0), x_prompt, pool_consts, state_rows=POOL_BUF,
        scratch=lambda tm, dd: [(POOL_HALO + tm, dd), (2, POOL_HALO + tm, dd // len(POOL_WINDOWS))],
        name="pool_p", casts=[(ffn_w_in, 0), (ffn_w_out, 0)])
    xs, pool_s = _pool_s(rows_major(x_sample), rows_major(state_pool[0]), pool_consts,
                         n_new=n_new, pos0=PAST_LEN)
    x, xs, (cw_in, cw_out, f_in, f_out) = ffn(
        x, xs, 0, f_in, f_out, [(conv_w_in, 0), (conv_w_out, 0), (ffn_w_in, 1), (ffn_w_out, 1)])

    conv_consts = (row(norm_mix[1]), cw_in, row(conv_b_in[0]), conv_dw[0], row(conv_dw_b[0]),
                   row(conv_ln_g[0]), row(conv_ln_b[0]), cw_out)
    x, conv_p = _conv_p(x, conv_consts)
    xs, conv_s = _conv_s(xs, rows_major(state_conv[0]), conv_consts, n_new=n_new)
    x, xs, (gw_in, gw_out, f_in, f_out) = ffn(
        x, xs, 1, f_in, f_out, [(gm_w_in, 0), (gm_w_out, 0), (ffn_w_in, 2), (ffn_w_out, 2)])

    gm_head = (row(norm_mix[2]), gw_in, row(gm_ln_g[0]), row(gm_ln_b[0]))
    x, v_p = _prompt_mixer(
        _gm_p_kernel, x, gm_head + (gm_w_s[0], gm_b_s[0].T, gw_out), state_rows=CHUNK,
        scratch=lambda tm, dd: [], name="gm_p")
    wrow = jnp.repeat(jnp.transpose(gm_w_s[0][:, :n_new, :n_new], (1, 2, 0)).reshape(n_new * n_new, N_GM_GROUPS), grp, axis=1)
    brow = jnp.repeat(gm_b_s[0][:, :n_new].T, grp, axis=1)
    xs, v_s = _single_step(
        functools.partial(_gm_s_kernel, n_new=n_new, nb=nb),
        (xs,) + gm_head + (wrow, brow, gw_out),
        [tm_shape, jax.ShapeDtypeStruct((nb, n_new * d), F32)],
        [], "gm_s")
    x, xs, (sw_in, sw_out, f_in, f_out) = ffn(
        x, xs, 2, f_in, f_out, [(sc_w_in, 0), (sc_w_out, 0), (ffn_w_in, 3), (ffn_w_out, 3)])

    sc_consts = (row(norm_mix[3]), sw_in, sc_conv[0], sw_out)
    x, sc_p = _prompt_mixer(
        _sc_p_kernel, x, sc_consts, state_rows=SC_BUF,
        scratch=lambda tm, dd: [(SC_HALO + tm, dd)], name="sc_p")
    xs, sc_s = _single_step(
        functools.partial(_sc_s_kernel, n_new=n_new),
        (xs, state_shortconv[0].reshape(nb, SC_BUF * d)) + sc_consts,
        [tm_shape, jax.ShapeDtypeStruct((nb, SC_BUF * d), F32)],
        [], "sc_s")
    y_prompt, ys, _ = ffn(x, xs, 3, f_in, f_out, [])
    y_sample = jnp.transpose(ys.reshape(n_new, nb, d), (1, 0, 2))

    return (y_prompt, y_sample,
            pool_p[None], rows_major(pool_s)[None],
            conv_p[None], rows_major(conv_s)[None],
            v_p[None], v_s.reshape(1, nb, n_new, d),
            sc_p[None], sc_s.reshape(1, nb, SC_BUF, d))
```
